```python
import math
import jax, jax.numpy as jnp
from jax import lax
import numpy as np


D_MODEL = 1024
BATCH = 4
SEQ = 4096
DEPTH = 2
DEC_BATCH = 8
DEC_SEQ = 2048
PAST_LEN = 128

GRID_W = 64
HEAD_DIM = 64
N_HEADS = 8
N_KV_HEADS = 2
Q_PER_KV = N_HEADS // N_KV_HEADS
ATTN_W = N_HEADS * HEAD_DIM
KV_W = N_KV_HEADS * HEAD_DIM
CONV_W = 512
MIX_W = ATTN_W + CONV_W
D_IN = ATTN_W + 2 * KV_W + 3 * CONV_W
SPLITS = (ATTN_W, ATTN_W + KV_W, ATTN_W + 2 * KV_W, ATTN_W + 2 * KV_W + CONV_W, ATTN_W + 2 * KV_W + 2 * CONV_W)
D_FF = 2816
CONV_K = 3
PLE_DIM = 256
Q_BLOCK = 128
ROPE_THETA = 10000.0
EPS = 1e-6

kernel_name = "hybrid_bidir_attn_shortconv_encoder"


def rmsnorm(x, g):
    xf = x.astype(jnp.float32)
    y = xf * lax.rsqrt(jnp.mean(xf * xf, axis=-1, keepdims=True) + EPS) * g.astype(jnp.float32)
    return y.astype(x.dtype)


def dwconv3(x, w):
    xp = jnp.pad(x, ((0, 0), (1, 1), (0, 0)))
    return xp[:, :-2] * w[0] + xp[:, 1:-1] * w[1] + xp[:, 2:] * w[2]


def axial_rope_tables(seq_len):
    rows = seq_len // GRID_W
    row = jnp.repeat(jnp.arange(rows), GRID_W).astype(jnp.float32)
    col = jnp.tile(jnp.arange(GRID_W), rows).astype(jnp.float32)
    half = HEAD_DIM // 2
    freqs = ROPE_THETA ** (-jnp.arange(0, half, 2, dtype=jnp.float32) / half)
    ang_r = row[:, None] * freqs[None, :]
    ang_c = col[:, None] * freqs[None, :]
    ang = jnp.concatenate([ang_r, ang_r, ang_c, ang_c], axis=-1)
    return jnp.cos(ang), jnp.sin(ang)


def apply_axial_rope(x, cos, sin):
    q4 = HEAD_DIM // 4
    x1, x2, x3, x4 = x[..., :q4], x[..., q4:2 * q4], x[..., 2 * q4:3 * q4], x[..., 3 * q4:]
    rot = jnp.concatenate([-x2, x1, -x4, x3], axis=-1)
    c = cos[None, :, None, :].astype(x.dtype)
    s = sin[None, :, None, :].astype(x.dtype)
    return x * c + rot * s


def bidir_gqa(q, k, v):
    B, S = q.shape[0], q.shape[1]
    nb = S // Q_BLOCK
    scale = 1.0 / math.sqrt(HEAD_DIM)
    qb = q.reshape(B, nb, Q_BLOCK, N_KV_HEADS, Q_PER_KV, HEAD_DIM).transpose(1, 0, 2, 3, 4, 5)

    def block(qi):
        s = jnp.einsum('bqkgd,bskd->bkgqs', qi, k).astype(jnp.float32) * scale
        p = jax.nn.softmax(s, axis=-1).astype(v.dtype)
        return jnp.einsum('bkgqs,bskd->bqkgd', p, v)

    o = lax.map(block, qb)
    return o.transpose(1, 0, 2, 3, 4, 5).reshape(B, S, ATTN_W)


def trunk(x, p, attn_norm_g, w_in, q_norm_g, k_norm_g, mix_conv_w, attn_out_g, conv_out_g,
          w_out, ffn_norm_g, w_up, ffn_conv_w, w_down, w_ple_gate, w_ple_proj):
    B, S, _ = x.shape
    cos, sin = axial_rope_tables(S)
    h = x
    for i in range(DEPTH):
        a = rmsnorm(h, attn_norm_g[i])
        z = a @ w_in[i]
        q, k, v, gb, gc, u = jnp.split(z, SPLITS, axis=-1)
        q = q.reshape(B, S, N_HEADS, HEAD_DIM)
        k = k.reshape(B, S, N_KV_HEADS, HEAD_DIM)
        v = v.reshape(B, S, N_KV_HEADS, HEAD_DIM)
        q = apply_axial_rope(rmsnorm(q, q_norm_g[i]), cos, sin)
        k = apply_axial_rope(rmsnorm(k, k_norm_g[i]), cos, sin)
        o_attn = bidir_gqa(q, k, v)
        o_conv = gb * dwconv3(gc * u, mix_conv_w[i])
        mixed = jnp.concatenate([rmsnorm(o_attn, attn_out_g[i]), rmsnorm(o_conv, conv_out_g[i])], axis=-1)
        h = h + mixed @ w_out[i]
        f = dwconv3(rmsnorm(h, ffn_norm_g[i]) @ w_up[i], ffn_conv_w[i])
        g, up = jnp.split(f, 2, axis=-1)
        h = h + (jax.nn.silu(g) * up) @ w_down[i]
        h = h + jax.nn.sigmoid(h @ w_ple_gate[i]) * (p[i] @ w_ple_proj[i])
    return h


def setup_inputs(seed: int = 0) -> dict:
    key = jax.random.key(seed)
    ks = jax.random.split(key, 20)
    f32 = jnp.float32

    def nrm(k, shape, scale):
        return jax.random.normal(k, shape, f32) * scale

    def gain(k, shape):
        return 1.0 + 0.01 * jax.random.normal(k, shape, f32)

    return {
        "x_prompt": nrm(ks[0], (BATCH, SEQ, D_MODEL), 1.0),
        "x_sample": nrm(ks[1], (DEC_BATCH, DEC_SEQ, D_MODEL), 1.0),
        "p_prompt": nrm(ks[2], (DEPTH, BATCH, SEQ, PLE_DIM), 1.0),
        "p_sample": nrm(ks[3], (DEPTH, DEC_BATCH, DEC_SEQ, PLE_DIM), 1.0),
        "attn_norm_g": gain(ks[4], (DEPTH, D_MODEL)),
        "w_in": nrm(ks[5], (DEPTH, D_MODEL, D_IN), D_MODEL ** -0.5),
        "q_norm_g": gain(ks[6], (DEPTH, HEAD_DIM)),
        "k_norm_g": gain(ks[7], (DEPTH, HEAD_DIM)),
        "mix_conv_w": nrm(ks[8], (DEPTH, CONV_K, CONV_W), CONV_K ** -0.5),
        "attn_out_g": gain(ks[9], (DEPTH, ATTN_W)),
        "conv_out_g": gain(ks[10], (DEPTH, CONV_W)),
        "w_out": nrm(ks[11], (DEPTH, MIX_W, D_MODEL), MIX_W ** -0.5),
        "ffn_norm_g": gain(ks[12], (DEPTH, D_MODEL)),
        "w_up": nrm(ks[13], (DEPTH, D_MODEL, 2 * D_FF), D_MODEL ** -0.5),
        "ffn_conv_w": nrm(ks[14], (DEPTH, CONV_K, 2 * D_FF), CONV_K ** -0.5),
        "w_down": nrm(ks[15], (DEPTH, D_FF, D_MODEL), D_FF ** -0.5),
        "w_ple_gate": nrm(ks[16], (DEPTH, D_MODEL, D_MODEL), D_MODEL ** -0.5),
        "w_ple_proj": nrm(ks[17], (DEPTH, PLE_DIM, D_MODEL), PLE_DIM ** -0.5),
    }


def reference(x_prompt, x_sample, p_prompt, p_sample, attn_norm_g, w_in, q_norm_g, k_norm_g,
              mix_conv_w, attn_out_g, conv_out_g, w_out, ffn_norm_g, w_up, ffn_conv_w, w_down,
              w_ple_gate, w_ple_proj):
    y_prompt = trunk(x_prompt, p_prompt, attn_norm_g, w_in, q_norm_g, k_norm_g, mix_conv_w,
                     attn_out_g, conv_out_g, w_out, ffn_norm_g, w_up, ffn_conv_w, w_down,
                     w_ple_gate, w_ple_proj)
    y_sample = trunk(x_sample, p_sample, attn_norm_g, w_in, q_norm_g, k_norm_g, mix_conv_w,
                     attn_out_g, conv_out_g, w_out, ffn_norm_g, w_up, ffn_conv_w, w_down,
                     w_ple_gate, w_ple_proj)
    return (y_prompt, y_sample)
```

```python
import functools
import math

import jax
import jax.numpy as jnp
from jax import lax
from jax.experimental import pallas as pl
from jax.experimental.pallas import tpu as pltpu

D_MODEL = 1024
GRID_W = 64
HEAD_DIM = 64
N_HEADS = 8
N_KV_HEADS = 2
Q_PER_KV = N_HEADS // N_KV_HEADS
ATTN_W = N_HEADS * HEAD_DIM
KV_W = N_KV_HEADS * HEAD_DIM
CONV_W = 512
D_IN = ATTN_W + 2 * KV_W + 3 * CONV_W
D_FF = 2816
PLE_DIM = 256
ROPE_THETA = 10000.0
EPS = 1e-6

LANES = 128
F32_ROWS = 8
BF16_ROWS = 16
VMEM_LIMIT_BYTES = 56 * 1024 * 1024

TM_IN = 512
TQ = 256
TK = 512
TM_POST = 512
FF_CHUNK = 256
N_FF_CHUNKS = D_FF // FF_CHUNK

F32 = jnp.float32
BF16 = jnp.bfloat16


def _resident(shape):
    zeros = (0,) * len(shape)
    return pl.BlockSpec(shape, lambda *_: zeros, pipeline_mode=pl.Buffered(1))


def _rms(x, g):
    return x * lax.rsqrt(jnp.mean(x * x, axis=-1, keepdims=True) + EPS) * g


def _rope(x, cos, sin_up, sin_dn):
    cols = []
    for c in range(x.shape[1] // LANES):
        xc = x[:, c * LANES:(c + 1) * LANES]
        up = pltpu.roll(xc, LANES - HEAD_DIM // 4, 1)
        dn = pltpu.roll(xc, HEAD_DIM // 4, 1)
        cols.append(xc * cos + up * sin_up + dn * sin_dn)
    return jnp.concatenate(cols, axis=1) if len(cols) > 1 else cols[0]


def _edge_mask(rows, tm, is_first, is_last):
    r = lax.broadcasted_iota(jnp.int32, (rows, 1), 0)
    outside = jnp.logical_or(jnp.logical_and(r < F32_ROWS, is_first),
                             jnp.logical_and(r >= F32_ROWS + tm, is_last))
    return jnp.logical_not(outside)


def _dwconv3(ref, w, tm):
    return (ref[F32_ROWS - 1:F32_ROWS - 1 + tm, :] * w[0:1, :]
            + ref[F32_ROWS:F32_ROWS + tm, :] * w[1:2, :]
            + ref[F32_ROWS + 1:F32_ROWS + 1 + tm, :] * w[2:3, :])


def _mixer_in_kernel(h_ref, hp_ref, hn_ref, ng_ref, w_ref, qg_ref, kg_ref, cos_ref, su_ref, sd_ref,
                     grp_ref, cw_ref, cg_ref,
                     q_ref, kt_ref, v_ref, oc_ref, cu_ref):
    i = pl.program_id(1)
    tm = h_ref.shape[0]
    lo, hi = F32_ROWS, F32_ROWS + tm
    x = jnp.concatenate([hp_ref[...], h_ref[...], hn_ref[...]], axis=0)
    a = _rms(x, ng_ref[...]).astype(BF16)

    def proj(c0, c1):
        return jnp.dot(a, w_ref[:, c0:c1], preferred_element_type=F32)

    cos, s_up, s_dn = cos_ref[...], su_ref[...], sd_ref[...]

    def head_norm(z, g):
        w = z.shape[1]
        ms = jnp.dot((z * z).astype(BF16), grp_ref[:w, :w], preferred_element_type=F32)
        return z * lax.rsqrt(ms + EPS) * g

    zq = proj(0, ATTN_W)[lo:hi]
    q_ref[...] = _rope(head_norm(zq, qg_ref[...]), cos, s_up, s_dn).astype(BF16)

    zkv = proj(ATTN_W, ATTN_W + 2 * KV_W)[lo:hi]
    k = _rope(head_norm(zkv[:, :KV_W], kg_ref[...]), cos, s_up, s_dn)
    kt_ref[...] = k.T.astype(BF16)
    v_ref[...] = zkv[:, KV_W:].astype(BF16)

    c0 = ATTN_W + 2 * KV_W
    gb = proj(c0, c0 + CONV_W)[lo:hi]
    cu = proj(c0 + CONV_W, c0 + 2 * CONV_W) * proj(c0 + 2 * CONV_W, c0 + 3 * CONV_W)
    keep = _edge_mask(cu.shape[0], tm, i == 0, i == pl.num_programs(1) - 1)
    cu_ref[...] = jnp.where(keep, cu, 0.0)
    oc = gb * _dwconv3(cu_ref, cw_ref[...], tm)
    oc_ref[...] = _rms(oc, cg_ref[...]).astype(BF16)


def _halo_specs(tm, seq, width, rows):
    per = tm // rows
    last = seq // rows - 1
    return [
        pl.BlockSpec((None, tm, width), lambda b, i: (b, i, 0)),
        pl.BlockSpec((None, rows, width), lambda b, i: (b, jnp.maximum(i * per - 1, 0), 0)),
        pl.BlockSpec((None, rows, width), lambda b, i: (b, jnp.minimum((i + 1) * per, last), 0)),
    ]


def _mixer_in(h, lw, tables):
    bsz, seq, _ = h.shape
    tm = min(TM_IN, seq)
    cos, s_up, s_dn = tables
    tab_spec = pl.BlockSpec((tm, LANES), lambda b, i: (i, 0))
    tok = lambda w: pl.BlockSpec((None, tm, w), lambda b, i: (b, i, 0))
    return pl.pallas_call(
        _mixer_in_kernel,
        grid=(bsz, seq // tm),
        in_specs=_halo_specs(tm, seq, D_MODEL, F32_ROWS) + [
            _resident((1, D_MODEL)), _resident((D_MODEL, D_IN)),
            _resident((1, ATTN_W)), _resident((1, KV_W)),
            tab_spec, tab_spec, tab_spec,
            _resident((ATTN_W, ATTN_W)), _resident((3, CONV_W)), _resident((1, CONV_W)),
        ],
        out_specs=[tok(ATTN_W),
                   pl.BlockSpec((None, KV_W, tm), lambda b, i: (b, 0, i)),
                   tok(KV_W), tok(CONV_W)],
        out_shape=[jax.ShapeDtypeStruct((bsz, seq, ATTN_W), BF16),
                   jax.ShapeDtypeStruct((bsz, KV_W, seq), BF16),
                   jax.ShapeDtypeStruct((bsz, seq, KV_W), BF16),
                   jax.ShapeDtypeStruct((bsz, seq, CONV_W), BF16)],
        scratch_shapes=[pltpu.VMEM((tm + 2 * F32_ROWS, CONV_W), F32)],
        compiler_params=pltpu.CompilerParams(
            dimension_semantics=("parallel", "arbitrary"), vmem_limit_bytes=VMEM_LIMIT_BYTES),
        name="mixer_in",
    )(h, h, h, lw["attn_norm_g"], lw["w_in"], lw["q_gain"], lw["k_gain"], cos, s_up, s_dn,
      lw["head_group"], lw["mix_conv_w"], lw["conv_out_g"])


def _attention_kernel(q_ref, kt_ref, v_ref, og_ref, o_ref, vlo_ref, vhi_ref, acc_ref):
    tq = q_ref.shape[0]
    seq = v_ref.shape[0]
    lane = lax.broadcasted_iota(jnp.int32, (1, LANES), 1)
    low_half = lane < HEAD_DIM

    @pl.when(pl.program_id(1) == 0)
    def _():
        v = v_ref[...]
        vs = pltpu.roll(v.astype(F32), HEAD_DIM, 1).astype(BF16)
        zero = jnp.zeros_like(v)
        vlo_ref[0] = jnp.where(low_half, v, zero)
        vhi_ref[0] = jnp.where(low_half, zero, vs)
        vlo_ref[1] = jnp.where(low_half, vs, zero)
        vhi_ref[1] = jnp.where(low_half, zero, v)

    n_chunks = seq // TK
    for pair in range(N_HEADS // 2):
        g = (2 * pair) // Q_PER_KV
        q_pair = q_ref[:, pair * LANES:(pair + 1) * LANES]
        q_even, q_odd = q_pair[:, :HEAD_DIM], q_pair[:, HEAD_DIM:]

        def body(c, carry, g=g, q_even=q_even, q_odd=q_odd):
            m_e, l_e, m_o, l_o, acc = carry
            start = pl.multiple_of(c * TK, TK)
            kt = kt_ref[g * HEAD_DIM:(g + 1) * HEAD_DIM, pl.ds(start, TK)]

            def one(qh, m, l):
                s = jnp.dot(qh, kt, preferred_element_type=F32)
                m_new = jnp.maximum(m, jnp.max(s, axis=-1, keepdims=True))
                alpha = jnp.exp(m - m_new)
                p = jnp.exp(s - m_new)
                return m_new, alpha * l + jnp.sum(p, axis=-1, keepdims=True), alpha, p.astype(BF16)

            m_e, l_e, a_e, p_e = one(q_even, m_e, l_e)
            m_o, l_o, a_o, p_o = one(q_odd, m_o, l_o)
            pv = (jnp.dot(p_e, vlo_ref[g, pl.ds(start, TK), :], preferred_element_type=F32)
                  + jnp.dot(p_o, vhi_ref[g, pl.ds(start, TK), :], preferred_element_type=F32))
            acc = acc * jnp.where(low_half, a_e, a_o) + pv
            return m_e, l_e, m_o, l_o, acc

        neg = jnp.full((tq, 1), -jnp.inf, F32)
        zero = jnp.zeros((tq, 1), F32)
        _, l_e, _, l_o, acc = lax.fori_loop(
            0, n_chunks, body, (neg, zero, neg, zero, jnp.zeros((tq, LANES), F32)))
        acc_ref[:, pair * LANES:(pair + 1) * LANES] = acc / jnp.where(low_half, l_e, l_o)

    o_ref[...] = _rms(acc_ref[...], og_ref[...]).astype(BF16)


def _attention(q, kt, v, out_gain):
    bsz, seq, _ = q.shape
    tq = min(TQ, seq)
    return pl.pallas_call(
        _attention_kernel,
        grid=(bsz, seq // tq),
        in_specs=[pl.BlockSpec((None, tq, ATTN_W), lambda b, i: (b, i, 0)),
                  pl.BlockSpec((None, KV_W, seq), lambda b, i: (b, 0, 0)),
                  pl.BlockSpec((None, seq, KV_W), lambda b, i: (b, 0, 0)),
                  _resident((1, ATTN_W))],
        out_specs=pl.BlockSpec((None, tq, ATTN_W), lambda b, i: (b, i, 0)),
        out_shape=jax.ShapeDtypeStruct((bsz, seq, ATTN_W), BF16),
        scratch_shapes=[pltpu.VMEM((N_KV_HEADS, seq, KV_W), BF16),
                        pltpu.VMEM((N_KV_HEADS, seq, KV_W), BF16),
                        pltpu.VMEM((tq, ATTN_W), F32)],
        compiler_params=pltpu.CompilerParams(
            dimension_semantics=("parallel", "arbitrary"), vmem_limit_bytes=VMEM_LIMIT_BYTES),
        name="attention",
    )(q, kt, v, out_gain)


def _post_kernel(h_ref, hp_ref, hn_ref, oa_ref, oap_ref, oan_ref, oc_ref, ocp_ref, ocn_ref, p_ref,
                 wo_ref, fg_ref, wup_ref, fcw_ref, wdn_ref, wpg_ref, wpp_ref,
                 y_ref, hm_ref, a2_ref, ug_ref, uu_ref, acc_ref):
    i = pl.program_id(1)
    tm = h_ref.shape[0]
    lo, hi = F32_ROWS, F32_ROWS + tm

    def extended(main_ref, prev_ref, next_ref):
        prev = prev_ref[...].astype(F32)[BF16_ROWS - F32_ROWS:]
        nxt = next_ref[...].astype(F32)[:F32_ROWS]
        return jnp.concatenate([prev, main_ref[...].astype(F32), nxt], axis=0).astype(BF16)

    oa = extended(oa_ref, oap_ref, oan_ref)
    oc = extended(oc_ref, ocp_ref, ocn_ref)
    x = jnp.concatenate([hp_ref[...], h_ref[...], hn_ref[...]], axis=0)
    hm = (x + jnp.dot(oa, wo_ref[:ATTN_W, :], preferred_element_type=F32)
          + jnp.dot(oc, wo_ref[ATTN_W:, :], preferred_element_type=F32))
    hm_ref[...] = hm[lo:hi]
    keep = _edge_mask(hm.shape[0], tm, i == 0, i == pl.num_programs(1) - 1)
    a2_ref[...] = jnp.where(keep, _rms(hm, fg_ref[...]), 0.0).astype(BF16)
    acc_ref[...] = jnp.zeros_like(acc_ref)

    def ff_chunk(c, carry):
        a2 = a2_ref[...]
        ug_ref[...] = jnp.dot(a2, wup_ref[0, c], preferred_element_type=F32)
        uu_ref[...] = jnp.dot(a2, wup_ref[1, c], preferred_element_type=F32)
        gate = _dwconv3(ug_ref, fcw_ref[0, c], tm)
        up = _dwconv3(uu_ref, fcw_ref[1, c], tm)
        act = (gate * jax.nn.sigmoid(gate) * up).astype(BF16)
        acc_ref[...] += jnp.dot(act, wdn_ref[c], preferred_element_type=F32)
        return carry

    lax.fori_loop(0, N_FF_CHUNKS, ff_chunk, 0)

    h2 = hm_ref[...] + acc_ref[...]
    gate = jax.nn.sigmoid(jnp.dot(h2.astype(BF16), wpg_ref[...], preferred_element_type=F32))
    emb = jnp.dot(p_ref[...].astype(BF16), wpp_ref[...], preferred_element_type=F32)
    y_ref[...] = h2 + gate * emb


def _post(h, o_attn, o_conv, p, lw):
    bsz, seq, _ = h.shape
    tm = min(TM_POST, seq)
    ext = tm + 2 * F32_ROWS
    return pl.pallas_call(
        _post_kernel,
        grid=(bsz, seq // tm),
        in_specs=(_halo_specs(tm, seq, D_MODEL, F32_ROWS)
                  + _halo_specs(tm, seq, ATTN_W, BF16_ROWS)
                  + _halo_specs(tm, seq, CONV_W, BF16_ROWS)
                  + [pl.BlockSpec((None, tm, PLE_DIM), lambda b, i: (b, i, 0)),
                     _resident((ATTN_W + CONV_W, D_MODEL)), _resident((1, D_MODEL)),
                     _resident((2, N_FF_CHUNKS, D_MODEL, FF_CHUNK)),
                     _resident((2, N_FF_CHUNKS, 3, FF_CHUNK)),
                     _resident((N_FF_CHUNKS, FF_CHUNK, D_MODEL)),
                     _resident((D_MODEL, D_MODEL)), _resident((PLE_DIM, D_MODEL))]),
        out_specs=pl.BlockSpec((None, tm, D_MODEL), lambda b, i: (b, i, 0)),
        out_shape=jax.ShapeDtypeStruct((bsz, seq, D_MODEL), F32),
        scratch_shapes=[pltpu.VMEM((tm, D_MODEL), F32),
                        pltpu.VMEM((ext, D_MODEL), BF16),
                        pltpu.VMEM((ext, FF_CHUNK), F32),
                        pltpu.VMEM((ext, FF_CHUNK), F32),
                        pltpu.VMEM((tm, D_MODEL), F32)],
        compiler_params=pltpu.CompilerParams(
            dimension_semantics=("parallel", "arbitrary"), vmem_limit_bytes=VMEM_LIMIT_BYTES),
        name="post",
    )(h, h, h, o_attn, o_attn, o_attn, o_conv, o_conv, o_conv, p,
      lw["w_out"], lw["ffn_norm_g"], lw["w_up"], lw["ffn_conv_w"], lw["w_down"],
      lw["w_ple_gate"], lw["w_ple_proj"])


def _rope_tables(seq):
    rows = seq // GRID_W
    row = jnp.repeat(jnp.arange(rows), GRID_W).astype(F32)
    col = jnp.tile(jnp.arange(GRID_W), rows).astype(F32)
    half = HEAD_DIM // 2
    freqs = ROPE_THETA ** (-jnp.arange(0, half, 2, dtype=F32) / half)
    ang_r = row[:, None] * freqs[None, :]
    ang_c = col[:, None] * freqs[None, :]
    ang = jnp.concatenate([ang_r, ang_r, ang_c, ang_c], axis=-1)
    cos, sin = jnp.cos(ang), jnp.sin(ang)
    first = (jnp.arange(HEAD_DIM) % (HEAD_DIM // 2)) < HEAD_DIM // 4
    sin_up = jnp.where(first, -sin, 0.0)
    sin_dn = jnp.where(first, 0.0, sin)
    widen = lambda t: jnp.tile(t, (1, LANES // HEAD_DIM))
    return widen(cos), widen(sin_up), widen(sin_dn)


def _layer_weights(i, attn_norm_g, w_in, q_norm_g, k_norm_g, mix_conv_w, attn_out_g, conv_out_g,
                   w_out, ffn_norm_g, w_up, ffn_conv_w, w_down, w_ple_gate, w_ple_proj):
    head = jnp.arange(ATTN_W) // HEAD_DIM
    chunked = lambda w, lead: w.reshape(lead, 2, N_FF_CHUNKS, FF_CHUNK).transpose(1, 2, 0, 3)
    return {
        "attn_norm_g": attn_norm_g[i][None],
        "w_in": w_in[i].astype(BF16),
        "q_gain": jnp.tile(q_norm_g[i] * (1.0 / math.sqrt(HEAD_DIM)), N_HEADS)[None],
        "k_gain": jnp.tile(k_norm_g[i], N_KV_HEADS)[None],
        "head_group": jnp.where(head[:, None] == head[None, :], 1.0 / HEAD_DIM, 0.0).astype(BF16),
        "mix_conv_w": mix_conv_w[i],
        "attn_out_g": attn_out_g[i][None],
        "conv_out_g": conv_out_g[i][None],
        "w_out": w_out[i].astype(BF16),
        "ffn_norm_g": ffn_norm_g[i][None],
        "w_up": chunked(w_up[i], D_MODEL).astype(BF16),
        "ffn_conv_w": chunked(ffn_conv_w[i], 3),
        "w_down": w_down[i].astype(BF16).reshape(N_FF_CHUNKS, FF_CHUNK, D_MODEL),
        "w_ple_gate": w_ple_gate[i].astype(BF16),
        "w_ple_proj": w_ple_proj[i].astype(BF16),
    }


def _trunk(x, p, layers):
    tables = _rope_tables(x.shape[1])
    h = x
    for i, lw in enumerate(layers):
        q, kt, v, o_conv = _mixer_in(h, lw, tables)
        o_attn = _attention(q, kt, v, lw["attn_out_g"])
        h = _post(h, o_attn, o_conv, p[i], lw)
    return h


def kernel(x_prompt, x_sample, p_prompt, p_sample, attn_norm_g, w_in, q_norm_g, k_norm_g, mix_conv_w,
           attn_out_g, conv_out_g, w_out, ffn_norm_g, w_up, ffn_conv_w, w_down, w_ple_gate, w_ple_proj):
    depth = w_in.shape[0]
    layers = [_layer_weights(i, attn_norm_g, w_in, q_norm_g, k_norm_g, mix_conv_w, attn_out_g,
                             conv_out_g, w_out, ffn_norm_g, w_up, ffn_conv_w, w_down, w_ple_gate,
                             w_ple_proj) for i in range(depth)]
    return (_trunk(x_prompt, p_prompt, layers), _trunk(x_sample, p_sample, layers))
```

```python
import functools
import math

import jax
import jax.numpy as jnp
from jax import lax
from jax.experimental import pallas as pl
from jax.experimental.pallas import tpu as pltpu

D_MODEL = 1024
GRID_W = 64
HEAD_DIM = 64
N_HEADS = 8
N_KV_HEADS = 2
Q_PER_KV = N_HEADS // N_KV_HEADS
ATTN_W = N_HEADS * HEAD_DIM
KV_W = N_KV_HEADS * HEAD_DIM
CONV_W = 512
D_IN = ATTN_W + 2 * KV_W + 3 * CONV_W
D_FF = 2816
PLE_DIM = 256
ROPE_THETA = 10000.0
EPS = 1e-6

LANES = 128
F32_ROWS = 8
BF16_ROWS = 16
VMEM_LIMIT_BYTES = 56 * 1024 * 1024

TM_IN = 512
TQ = 256
TK = 512
SCORE_LOOKAHEAD = 2
V_ROWS = HEAD_DIM + BF16_ROWS
TM_POST = 512
FF_CHUNK = 256
N_FF_CHUNKS = D_FF // FF_CHUNK

F32 = jnp.float32
BF16 = jnp.bfloat16


def _resident(shape):
    zeros = (0,) * len(shape)
    return pl.BlockSpec(shape, lambda *_: zeros, pipeline_mode=pl.Buffered(1))


def _rms(x, g):
    return x * lax.rsqrt(jnp.mean(x * x, axis=-1, keepdims=True) + EPS) * g


def _rope(x, cos, sin_up, sin_dn):
    cols = []
    for c in range(x.shape[1] // LANES):
        xc = x[:, c * LANES:(c + 1) * LANES]
        up = pltpu.roll(xc, LANES - HEAD_DIM // 4, 1)
        dn = pltpu.roll(xc, HEAD_DIM // 4, 1)
        cols.append(xc * cos + up * sin_up + dn * sin_dn)
    return jnp.concatenate(cols, axis=1) if len(cols) > 1 else cols[0]


def _edge_mask(rows, tm, is_first, is_last):
    r = lax.broadcasted_iota(jnp.int32, (rows, 1), 0)
    outside = jnp.logical_or(jnp.logical_and(r < F32_ROWS, is_first),
                             jnp.logical_and(r >= F32_ROWS + tm, is_last))
    return jnp.logical_not(outside)


def _dwconv3(ref, w, tm):
    return (ref[F32_ROWS - 1:F32_ROWS - 1 + tm, :] * w[0:1, :]
            + ref[F32_ROWS:F32_ROWS + tm, :] * w[1:2, :]
            + ref[F32_ROWS + 1:F32_ROWS + 1 + tm, :] * w[2:3, :])


def _mixer_in_kernel(h_ref, hp_ref, hn_ref, ng_ref, w_ref, qg_ref, kg_ref, cos_ref, su_ref, sd_ref,
                     grp_ref, cw_ref, cg_ref,
                     qt_ref, k_ref, vt_ref, oc_ref, cu_ref):
    i = pl.program_id(1)
    tm = h_ref.shape[0]
    lo, hi = F32_ROWS, F32_ROWS + tm
    x = jnp.concatenate([hp_ref[...], h_ref[...], hn_ref[...]], axis=0)
    a = _rms(x, ng_ref[...]).astype(BF16)

    def proj(c0, c1):
        return jnp.dot(a, w_ref[:, c0:c1], preferred_element_type=F32)

    cos, s_up, s_dn = cos_ref[...], su_ref[...], sd_ref[...]

    def head_norm(z, g):
        w = z.shape[1]
        ms = jnp.dot((z * z).astype(BF16), grp_ref[:w, :w], preferred_element_type=F32)
        return z * lax.rsqrt(ms + EPS) * g

    zq = proj(0, ATTN_W)[lo:hi]
    qt_ref[...] = _rope(head_norm(zq, qg_ref[...]), cos, s_up, s_dn).T.astype(BF16)

    zkv = proj(ATTN_W, ATTN_W + 2 * KV_W)[lo:hi]
    k_ref[...] = _rope(head_norm(zkv[:, :KV_W], kg_ref[...]), cos, s_up, s_dn).astype(BF16)
    vt = zkv[:, KV_W:].T.astype(BF16)
    for g in range(N_KV_HEADS):
        vt_ref[g * V_ROWS:g * V_ROWS + HEAD_DIM, :] = vt[g * HEAD_DIM:(g + 1) * HEAD_DIM]
        vt_ref[g * V_ROWS + HEAD_DIM:(g + 1) * V_ROWS, :] = jnp.ones((BF16_ROWS, tm), BF16)

    c0 = ATTN_W + 2 * KV_W
    gb = proj(c0, c0 + CONV_W)[lo:hi]
    cu = proj(c0 + CONV_W, c0 + 2 * CONV_W) * proj(c0 + 2 * CONV_W, c0 + 3 * CONV_W)
    keep = _edge_mask(cu.shape[0], tm, i == 0, i == pl.num_programs(1) - 1)
    cu_ref[...] = jnp.where(keep, cu, 0.0)
    oc = gb * _dwconv3(cu_ref, cw_ref[...], tm)
    oc_ref[...] = _rms(oc, cg_ref[...]).astype(BF16)


def _halo_specs(tm, seq, width, rows):
    per = tm // rows
    last = seq // rows - 1
    return [
        pl.BlockSpec((None, tm, width), lambda b, i: (b, i, 0)),
        pl.BlockSpec((None, rows, width), lambda b, i: (b, jnp.maximum(i * per - 1, 0), 0)),
        pl.BlockSpec((None, rows, width), lambda b, i: (b, jnp.minimum((i + 1) * per, last), 0)),
    ]


def _mixer_in(h, lw, tables):
    bsz, seq, _ = h.shape
    tm = min(TM_IN, seq)
    cos, s_up, s_dn = tables
    tab_spec = pl.BlockSpec((tm, LANES), lambda b, i: (i, 0))
    tok = lambda w: pl.BlockSpec((None, tm, w), lambda b, i: (b, i, 0))
    tok_t = lambda w: pl.BlockSpec((None, w, tm), lambda b, i: (b, 0, i))
    return pl.pallas_call(
        _mixer_in_kernel,
        grid=(bsz, seq // tm),
        in_specs=_halo_specs(tm, seq, D_MODEL, F32_ROWS) + [
            _resident((1, D_MODEL)), _resident((D_MODEL, D_IN)),
            _resident((1, ATTN_W)), _resident((1, KV_W)),
            tab_spec, tab_spec, tab_spec,
            _resident((ATTN_W, ATTN_W)), _resident((3, CONV_W)), _resident((1, CONV_W)),
        ],
        out_specs=[tok_t(ATTN_W), tok(KV_W), tok_t(N_KV_HEADS * V_ROWS), tok(CONV_W)],
        out_shape=[jax.ShapeDtypeStruct((bsz, ATTN_W, seq), BF16),
                   jax.ShapeDtypeStruct((bsz, seq, KV_W), BF16),
                   jax.ShapeDtypeStruct((bsz, N_KV_HEADS * V_ROWS, seq), BF16),
                   jax.ShapeDtypeStruct((bsz, seq, CONV_W), BF16)],
        scratch_shapes=[pltpu.VMEM((tm + 2 * F32_ROWS, CONV_W), F32)],
        compiler_params=pltpu.CompilerParams(
            dimension_semantics=("parallel", "arbitrary"), vmem_limit_bytes=VMEM_LIMIT_BYTES),
        name="mixer_in",
    )(h, h, h, lw["attn_norm_g"], lw["w_in"], lw["q_gain"], lw["k_gain"], cos, s_up, s_dn,
      lw["head_group"], lw["mix_conv_w"], lw["conv_out_g"])


def _attention_kernel(qt_ref, k_ref, vt_ref, og_ref, o_ref, qpad_ref, m_ref, acc_ref, ot_ref, stage_ref):
    seq = k_ref.shape[0]
    n_chunks = seq // TK
    half = TK // 2

    qpad_ref[...] = jnp.zeros_like(qpad_ref)
    for h in range(N_HEADS):
        g = h // Q_PER_KV
        qpad_ref[h, g * HEAD_DIM:(g + 1) * HEAD_DIM, :] = qt_ref[h * HEAD_DIM:(h + 1) * HEAD_DIM, :]
    m_ref[...] = jnp.full_like(m_ref, -jnp.inf)
    acc_ref[...] = jnp.zeros_like(acc_ref)

    def scores(c, h):
        start = pl.multiple_of(c * TK, TK)
        return [jnp.dot(k_ref[pl.ds(start + j * half, half), :], qpad_ref[h],
                        preferred_element_type=F32) for j in range(2)]

    def stage(slot, st):
        stage_ref[slot, 0] = st[0]
        stage_ref[slot, 1] = st[1]

    for h in range(SCORE_LOOKAHEAD):
        stage(h, scores(0, h))

    def chunk(c, carry):
        start = pl.multiple_of(c * TK, TK)
        c_next = jnp.minimum(c + 1, n_chunks - 1)
        pending = [(lambda j, h=h: stage_ref[h, j]) for h in range(SCORE_LOOKAHEAD)]
        for h in range(N_HEADS):
            g = h // Q_PER_KV
            st = pending.pop(0)
            ahead = h + SCORE_LOOKAHEAD
            if ahead < N_HEADS:
                pending.append(scores(c, ahead).__getitem__)
            m_old = m_ref[h]
            m_new = jnp.maximum(m_old, jnp.maximum(jnp.max(st(0), axis=0, keepdims=True),
                                                   jnp.max(st(1), axis=0, keepdims=True)))
            alpha = jnp.exp2(m_old - m_new)
            pb = jnp.concatenate([jnp.exp2(st(j) - m_new).astype(BF16) for j in range(2)], axis=0)
            if ahead >= N_HEADS:
                stage(ahead - N_HEADS, scores(c_next, ahead - N_HEADS))
            vt = vt_ref[g * V_ROWS:(g + 1) * V_ROWS, pl.ds(start, TK)]
            acc_ref[h] = alpha * acc_ref[h] + jnp.dot(vt, pb, preferred_element_type=F32)
            m_ref[h] = m_new
        return carry

    lax.fori_loop(0, n_chunks, chunk, 0)

    for h in range(N_HEADS):
        ot_ref[h * HEAD_DIM:(h + 1) * HEAD_DIM, :] = (
            acc_ref[h, :HEAD_DIM, :] / acc_ref[h, HEAD_DIM:HEAD_DIM + 1, :])
    o_ref[...] = _rms(ot_ref[...].T, og_ref[...]).astype(BF16)


def _attention(qt, k, vt, out_gain):
    bsz, seq, _ = k.shape
    tq = min(TQ, seq)
    return pl.pallas_call(
        _attention_kernel,
        grid=(bsz, seq // tq),
        in_specs=[pl.BlockSpec((None, ATTN_W, tq), lambda b, i: (b, 0, i)),
                  pl.BlockSpec((None, seq, KV_W), lambda b, i: (b, 0, 0)),
                  pl.BlockSpec((None, N_KV_HEADS * V_ROWS, seq), lambda b, i: (b, 0, 0)),
                  _resident((1, ATTN_W))],
        out_specs=pl.BlockSpec((None, tq, ATTN_W), lambda b, i: (b, i, 0)),
        out_shape=jax.ShapeDtypeStruct((bsz, seq, ATTN_W), BF16),
        scratch_shapes=[pltpu.VMEM((N_HEADS, KV_W, tq), BF16),
                        pltpu.VMEM((N_HEADS, 1, tq), F32),
                        pltpu.VMEM((N_HEADS, V_ROWS, tq), F32),
                        pltpu.VMEM((ATTN_W, tq), F32),
                        pltpu.VMEM((SCORE_LOOKAHEAD, 2, TK // 2, tq), F32)],
        compiler_params=pltpu.CompilerParams(
            dimension_semantics=("parallel", "arbitrary"), vmem_limit_bytes=VMEM_LIMIT_BYTES),
        name="attention",
    )(qt, k, vt, out_gain)


def _post_kernel(h_ref, hp_ref, hn_ref, oa_ref, oap_ref, oan_ref, oc_ref, ocp_ref, ocn_ref, p_ref,
                 wo_ref, fg_ref, wup_ref, fcw_ref, wdn_ref, wpg_ref, wpp_ref,
                 y_ref, hm_ref, a2_ref, ug_ref, uu_ref, acc_ref):
    i = pl.program_id(1)
    tm = h_ref.shape[0]
    lo, hi = F32_ROWS, F32_ROWS + tm

    def extended(main_ref, prev_ref, next_ref):
        prev = prev_ref[...].astype(F32)[BF16_ROWS - F32_ROWS:]
        nxt = next_ref[...].astype(F32)[:F32_ROWS]
        return jnp.concatenate([prev, main_ref[...].astype(F32), nxt], axis=0).astype(BF16)

    oa = extended(oa_ref, oap_ref, oan_ref)
    oc = extended(oc_ref, ocp_ref, ocn_ref)
    x = jnp.concatenate([hp_ref[...], h_ref[...], hn_ref[...]], axis=0)
    hm = (x + jnp.dot(oa, wo_ref[:ATTN_W, :], preferred_element_type=F32)
          + jnp.dot(oc, wo_ref[ATTN_W:, :], preferred_element_type=F32))
    hm_ref[...] = hm[lo:hi]
    keep = _edge_mask(hm.shape[0], tm, i == 0, i == pl.num_programs(1) - 1)
    a2_ref[...] = jnp.where(keep, _rms(hm, fg_ref[...]), 0.0).astype(BF16)
    acc_ref[...] = jnp.zeros_like(acc_ref)

    def ff_chunk(c, carry):
        a2 = a2_ref[...]
        ug_ref[...] = jnp.dot(a2, wup_ref[0, c], preferred_element_type=F32)
        uu_ref[...] = jnp.dot(a2, wup_ref[1, c], preferred_element_type=F32)
        gate = _dwconv3(ug_ref, fcw_ref[0, c], tm)
        up = _dwconv3(uu_ref, fcw_ref[1, c], tm)
        act = (gate * jax.nn.sigmoid(gate) * up).astype(BF16)
        acc_ref[...] += jnp.dot(act, wdn_ref[c], preferred_element_type=F32)
        return carry

    lax.fori_loop(0, N_FF_CHUNKS, ff_chunk, 0)

    h2 = hm_ref[...] + acc_ref[...]
    gate = jax.nn.sigmoid(jnp.dot(h2.astype(BF16), wpg_ref[...], preferred_element_type=F32))
    emb = jnp.dot(p_ref[...].astype(BF16), wpp_ref[...], preferred_element_type=F32)
    y_ref[...] = h2 + gate * emb


def _post(h, o_attn, o_conv, p, lw):
    bsz, seq, _ = h.shape
    tm = min(TM_POST, seq)
    ext = tm + 2 * F32_ROWS
    return pl.pallas_call(
        _post_kernel,
        grid=(bsz, seq // tm),
        in_specs=(_halo_specs(tm, seq, D_MODEL, F32_ROWS)
                  + _halo_specs(tm, seq, ATTN_W, BF16_ROWS)
                  + _halo_specs(tm, seq, CONV_W, BF16_ROWS)
                  + [pl.BlockSpec((None, tm, PLE_DIM), lambda b, i: (b, i, 0)),
                     _resident((ATTN_W + CONV_W, D_MODEL)), _resident((1, D_MODEL)),
                     _resident((2, N_FF_CHUNKS, D_MODEL, FF_CHUNK)),
                     _resident((2, N_FF_CHUNKS, 3, FF_CHUNK)),
                     _resident((N_FF_CHUNKS, FF_CHUNK, D_MODEL)),
                     _resident((D_MODEL, D_MODEL)), _resident((PLE_DIM, D_MODEL))]),
        out_specs=pl.BlockSpec((None, tm, D_MODEL), lambda b, i: (b, i, 0)),
        out_shape=jax.ShapeDtypeStruct((bsz, seq, D_MODEL), F32),
        scratch_shapes=[pltpu.VMEM((tm, D_MODEL), F32),
                        pltpu.VMEM((ext, D_MODEL), BF16),
                        pltpu.VMEM((ext, FF_CHUNK), F32),
                        pltpu.VMEM((ext, FF_CHUNK), F32),
                        pltpu.VMEM((tm, D_MODEL), F32)],
        compiler_params=pltpu.CompilerParams(
            dimension_semantics=("parallel", "arbitrary"), vmem_limit_bytes=VMEM_LIMIT_BYTES),
        name="post",
    )(h, h, h, o_attn, o_attn, o_attn, o_conv, o_conv, o_conv, p,
      lw["w_out"], lw["ffn_norm_g"], lw["w_up"], lw["ffn_conv_w"], lw["w_down"],
      lw["w_ple_gate"], lw["w_ple_proj"])


def _rope_tables(seq):
    rows = seq // GRID_W
    row = jnp.repeat(jnp.arange(rows), GRID_W).astype(F32)
    col = jnp.tile(jnp.arange(GRID_W), rows).astype(F32)
    half = HEAD_DIM // 2
    freqs = ROPE_THETA ** (-jnp.arange(0, half, 2, dtype=F32) / half)
    ang_r = row[:, None] * freqs[None, :]
    ang_c = col[:, None] * freqs[None, :]
    ang = jnp.concatenate([ang_r, ang_r, ang_c, ang_c], axis=-1)
    cos, sin = jnp.cos(ang), jnp.sin(ang)
    first = (jnp.arange(HEAD_DIM) % (HEAD_DIM // 2)) < HEAD_DIM // 4
    sin_up = jnp.where(first, -sin, 0.0)
    sin_dn = jnp.where(first, 0.0, sin)
    widen = lambda t: jnp.tile(t, (1, LANES // HEAD_DIM))
    return widen(cos), widen(sin_up), widen(sin_dn)


def _layer_weights(i, attn_norm_g, w_in, q_norm_g, k_norm_g, mix_conv_w, attn_out_g, conv_out_g,
                   w_out, ffn_norm_g, w_up, ffn_conv_w, w_down, w_ple_gate, w_ple_proj):
    head = jnp.arange(ATTN_W) // HEAD_DIM
    chunked = lambda w, lead: w.reshape(lead, 2, N_FF_CHUNKS, FF_CHUNK).transpose(1, 2, 0, 3)
    return {
        "attn_norm_g": attn_norm_g[i][None],
        "w_in": w_in[i].astype(BF16),
        "q_gain": jnp.tile(q_norm_g[i] * (math.log2(math.e) / math.sqrt(HEAD_DIM)), N_HEADS)[None],
        "k_gain": jnp.tile(k_norm_g[i], N_KV_HEADS)[None],
        "head_group": jnp.where(head[:, None] == head[None, :], 1.0 / HEAD_DIM, 0.0).astype(BF16),
        "mix_conv_w": mix_conv_w[i],
        "attn_out_g": attn_out_g[i][None],
        "conv_out_g": conv_out_g[i][None],
        "w_out": w_out[i].astype(BF16),
        "ffn_norm_g": ffn_norm_g[i][None],
        "w_up": chunked(w_up[i], D_MODEL).astype(BF16),
        "ffn_conv_w": chunked(ffn_conv_w[i], 3),
        "w_down": w_down[i].astype(BF16).reshape(N_FF_CHUNKS, FF_CHUNK, D_MODEL),
        "w_ple_gate": w_ple_gate[i].astype(BF16),
        "w_ple_proj": w_ple_proj[i].astype(BF16),
    }


def _trunk(x, p, layers):
    tables = _rope_tables(x.shape[1])
    h = x
    for i, lw in enumerate(layers):
        qt, k, vt, o_conv = _mixer_in(h, lw, tables)
        o_attn = _attention(qt, k, vt, lw["attn_out_g"])
        h = _post(h, o_attn, o_conv, p[i], lw)
    return h


def kernel(x_prompt, x_sample, p_prompt, p_sample, attn_norm_g, w_in, q_norm_g, k_norm_g, mix_conv_w,
           attn_out_g, conv_out_g, w_out, ffn_norm_g, w_up, ffn_conv_w, w_down, w_ple_gate, w_ple_proj):
    depth = w_in.shape[0]
    layers = [_layer_weights(i, attn_norm_g, w_in, q_norm_g, k_norm_g, mix_conv_w, attn_out_g,
                             conv_out_g, w_out, ffn_norm_g, w_up, ffn_conv_w, w_down, w_ple_gate,
                             w_ple_proj) for i in range(depth)]
    return (_trunk(x_prompt, p_prompt, layers), _trunk(x_sample, p_sample, layers))
```

```python
import functools
import math

import jax
import jax.numpy as jnp
from jax import lax
from jax.experimental import pallas as pl
from jax.experimental.pallas import tpu as pltpu

D_MODEL = 1024
GRID_W = 64
HEAD_DIM = 64
N_HEADS = 8
N_KV_HEADS = 2
Q_PER_KV = N_HEADS // N_KV_HEADS
ATTN_W = N_HEADS * HEAD_DIM
KV_W = N_KV_HEADS * HEAD_DIM
CONV_W = 512
D_IN = ATTN_W + 2 * KV_W + 3 * CONV_W
D_FF = 2816
PLE_DIM = 256
ROPE_THETA = 10000.0
EPS = 1e-6

LANES = 128
F32_ROWS = 8
BF16_ROWS = 16
VMEM_LIMIT_BYTES = 56 * 1024 * 1024

TM_IN = 512
TQ = 256
TK = 512
SCORE_LOOKAHEAD = 2
V_ROWS = HEAD_DIM + BF16_ROWS
TM_POST = 512
FF_CHUNK = 256
N_FF_CHUNKS = D_FF // FF_CHUNK
assert N_FF_CHUNKS * FF_CHUNK == D_FF and N_FF_CHUNKS % 2 == 1

F32 = jnp.float32
BF16 = jnp.bfloat16


def _resident(shape):
    zeros = (0,) * len(shape)
    return pl.BlockSpec(shape, lambda *_: zeros, pipeline_mode=pl.Buffered(1))


def _rms(x, g):
    return x * lax.rsqrt(jnp.mean(x * x, axis=-1, keepdims=True) + EPS) * g


def _rope(x, cos, sin_up, sin_dn):
    cols = []
    for c in range(x.shape[1] // LANES):
        xc = x[:, c * LANES:(c + 1) * LANES]
        up = pltpu.roll(xc, LANES - HEAD_DIM // 4, 1)
        dn = pltpu.roll(xc, HEAD_DIM // 4, 1)
        cols.append(xc * cos + up * sin_up + dn * sin_dn)
    return jnp.concatenate(cols, axis=1) if len(cols) > 1 else cols[0]


def _edge_mask(rows, tm, is_first, is_last):
    r = lax.broadcasted_iota(jnp.int32, (rows, 1), 0)
    outside = jnp.logical_or(jnp.logical_and(r < F32_ROWS, is_first),
                             jnp.logical_and(r >= F32_ROWS + tm, is_last))
    return jnp.logical_not(outside)


def _dwconv3(ref, w, r0, nrows):
    x = ref[r0:r0 + nrows + 2 * F32_ROWS, :]
    lo = F32_ROWS
    return (pltpu.roll(x, 1, 0)[lo:lo + nrows] * w[0:1, :] + x[lo:lo + nrows] * w[1:2, :]
            + pltpu.roll(x, x.shape[0] - 1, 0)[lo:lo + nrows] * w[2:3, :])


def _mixer_in_kernel(h_ref, hp_ref, hn_ref, ng_ref, w_ref, qg_ref, kg_ref, cos_ref, su_ref, sd_ref,
                     grp_ref, cw_ref, cg_ref,
                     qt_ref, k_ref, vt_ref, oc_ref, cu_ref):
    i = pl.program_id(1)
    tm = h_ref.shape[0]
    lo, hi = F32_ROWS, F32_ROWS + tm
    x = jnp.concatenate([hp_ref[...], h_ref[...], hn_ref[...]], axis=0)
    a = _rms(x, ng_ref[...]).astype(BF16)

    def proj(c0, c1):
        return jnp.dot(a, w_ref[:, c0:c1], preferred_element_type=F32)

    cos, s_up, s_dn = cos_ref[...], su_ref[...], sd_ref[...]

    def head_norm(z, g):
        w = z.shape[1]
        ms = jnp.dot((z * z).astype(BF16), grp_ref[:w, :w], preferred_element_type=F32)
        return z * lax.rsqrt(ms + EPS) * g

    c0 = ATTN_W + 2 * KV_W
    cu = proj(c0 + CONV_W, c0 + 2 * CONV_W) * proj(c0 + 2 * CONV_W, c0 + 3 * CONV_W)
    keep = _edge_mask(cu.shape[0], tm, i == 0, i == pl.num_programs(1) - 1)
    cu_ref[...] = jnp.where(keep, cu, 0.0)

    zq = proj(0, ATTN_W)[lo:hi]
    conv = _dwconv3(cu_ref, cw_ref[...], 0, tm)
    qt_ref[...] = _rope(head_norm(zq, qg_ref[...]), cos, s_up, s_dn).T.astype(BF16)

    zkv = proj(ATTN_W, ATTN_W + 2 * KV_W)[lo:hi]
    k_ref[...] = _rope(head_norm(zkv[:, :KV_W], kg_ref[...]), cos, s_up, s_dn).astype(BF16)
    vt = zkv[:, KV_W:].T.astype(BF16)
    for g in range(N_KV_HEADS):
        vt_ref[g * V_ROWS:g * V_ROWS + HEAD_DIM, :] = vt[g * HEAD_DIM:(g + 1) * HEAD_DIM]
        vt_ref[g * V_ROWS + HEAD_DIM:(g + 1) * V_ROWS, :] = jnp.ones((BF16_ROWS, tm), BF16)

    gb = proj(c0, c0 + CONV_W)[lo:hi]
    oc_ref[...] = _rms(gb * conv, cg_ref[...]).astype(BF16)


def _halo_specs(tm, seq, width, rows):
    per = tm // rows
    last = seq // rows - 1
    return [
        pl.BlockSpec((None, tm, width), lambda b, i: (b, i, 0)),
        pl.BlockSpec((None, rows, width), lambda b, i: (b, jnp.maximum(i * per - 1, 0), 0)),
        pl.BlockSpec((None, rows, width), lambda b, i: (b, jnp.minimum((i + 1) * per, last), 0)),
    ]


def _mixer_in(h, lw, tables):
    bsz, seq, _ = h.shape
    tm = min(TM_IN, seq)
    cos, s_up, s_dn = tables
    tab_spec = pl.BlockSpec((tm, LANES), lambda b, i: (i, 0))
    tok = lambda w: pl.BlockSpec((None, tm, w), lambda b, i: (b, i, 0))
    tok_t = lambda w: pl.BlockSpec((None, w, tm), lambda b, i: (b, 0, i))
    return pl.pallas_call(
        _mixer_in_kernel,
        grid=(bsz, seq // tm),
        in_specs=_halo_specs(tm, seq, D_MODEL, F32_ROWS) + [
            _resident((1, D_MODEL)), _resident((D_MODEL, D_IN)),
            _resident((1, ATTN_W)), _resident((1, KV_W)),
            tab_spec, tab_spec, tab_spec,
            _resident((ATTN_W, ATTN_W)), _resident((3, CONV_W)), _resident((1, CONV_W)),
        ],
        out_specs=[tok_t(ATTN_W), tok(KV_W), tok_t(N_KV_HEADS * V_ROWS), tok(CONV_W)],
        out_shape=[jax.ShapeDtypeStruct((bsz, ATTN_W, seq), BF16),
                   jax.ShapeDtypeStruct((bsz, seq, KV_W), BF16),
                   jax.ShapeDtypeStruct((bsz, N_KV_HEADS * V_ROWS, seq), BF16),
                   jax.ShapeDtypeStruct((bsz, seq, CONV_W), BF16)],
        scratch_shapes=[pltpu.VMEM((tm + 2 * F32_ROWS, CONV_W), F32)],
        compiler_params=pltpu.CompilerParams(
            dimension_semantics=("parallel", "arbitrary"), vmem_limit_bytes=VMEM_LIMIT_BYTES),
        name="mixer_in",
    )(h, h, h, lw["attn_norm_g"], lw["w_in"], lw["q_gain"], lw["k_gain"], cos, s_up, s_dn,
      lw["head_group"], lw["mix_conv_w"], lw["conv_out_g"])


def _attention_kernel(qt_ref, k_ref, vt_ref, og_ref, o_ref, qpad_ref, m_ref, acc_ref, ot_ref, stage_ref):
    seq = k_ref.shape[0]
    n_chunks = seq // TK
    half = TK // 2

    qpad_ref[...] = jnp.zeros_like(qpad_ref)
    for h in range(N_HEADS):
        g = h // Q_PER_KV
        qpad_ref[h, g * HEAD_DIM:(g + 1) * HEAD_DIM, :] = qt_ref[h * HEAD_DIM:(h + 1) * HEAD_DIM, :]
    m_ref[...] = jnp.full_like(m_ref, -jnp.inf)
    acc_ref[...] = jnp.zeros_like(acc_ref)

    def scores(c, h):
        start = pl.multiple_of(c * TK, TK)
        return [jnp.dot(k_ref[pl.ds(start + j * half, half), :], qpad_ref[h],
                        preferred_element_type=F32) for j in range(2)]

    def stage(slot, st):
        stage_ref[slot, 0] = st[0]
        stage_ref[slot, 1] = st[1]

    for h in range(SCORE_LOOKAHEAD):
        stage(h, scores(0, h))

    def chunk(c, carry):
        start = pl.multiple_of(c * TK, TK)
        c_next = jnp.minimum(c + 1, n_chunks - 1)
        pending = [(lambda j, h=h: stage_ref[h, j]) for h in range(SCORE_LOOKAHEAD)]
        for h in range(N_HEADS):
            g = h // Q_PER_KV
            st = pending.pop(0)
            ahead = h + SCORE_LOOKAHEAD
            if ahead < N_HEADS:
                pending.append(scores(c, ahead).__getitem__)
            m_old = m_ref[h]
            m_new = jnp.maximum(m_old, jnp.maximum(jnp.max(st(0), axis=0, keepdims=True),
                                                   jnp.max(st(1), axis=0, keepdims=True)))
            alpha = jnp.exp2(m_old - m_new)
            pb = jnp.concatenate([jnp.exp2(st(j) - m_new).astype(BF16) for j in range(2)], axis=0)
            if ahead >= N_HEADS:
                stage(ahead - N_HEADS, scores(c_next, ahead - N_HEADS))
            vt = vt_ref[g * V_ROWS:(g + 1) * V_ROWS, pl.ds(start, TK)]
            acc_ref[h] = alpha * acc_ref[h] + jnp.dot(vt, pb, preferred_element_type=F32)
            m_ref[h] = m_new
        return carry

    lax.fori_loop(0, n_chunks, chunk, 0)

    for h in range(N_HEADS):
        ot_ref[h * HEAD_DIM:(h + 1) * HEAD_DIM, :] = (
            acc_ref[h, :HEAD_DIM, :] / acc_ref[h, HEAD_DIM:HEAD_DIM + 1, :])
    o_ref[...] = _rms(ot_ref[...].T, og_ref[...]).astype(BF16)


def _attention(qt, k, vt, out_gain):
    bsz, seq, _ = k.shape
    tq = min(TQ, seq)
    return pl.pallas_call(
        _attention_kernel,
        grid=(bsz, seq // tq),
        in_specs=[pl.BlockSpec((None, ATTN_W, tq), lambda b, i: (b, 0, i)),
                  pl.BlockSpec((None, seq, KV_W), lambda b, i: (b, 0, 0)),
                  pl.BlockSpec((None, N_KV_HEADS * V_ROWS, seq), lambda b, i: (b, 0, 0)),
                  _resident((1, ATTN_W))],
        out_specs=pl.BlockSpec((None, tq, ATTN_W), lambda b, i: (b, i, 0)),
        out_shape=jax.ShapeDtypeStruct((bsz, seq, ATTN_W), BF16),
        scratch_shapes=[pltpu.VMEM((N_HEADS, KV_W, tq), BF16),
                        pltpu.VMEM((N_HEADS, 1, tq), F32),
                        pltpu.VMEM((N_HEADS, V_ROWS, tq), F32),
                        pltpu.VMEM((ATTN_W, tq), F32),
                        pltpu.VMEM((SCORE_LOOKAHEAD, 2, TK // 2, tq), F32)],
        compiler_params=pltpu.CompilerParams(
            dimension_semantics=("parallel", "arbitrary"), vmem_limit_bytes=VMEM_LIMIT_BYTES),
        name="attention",
    )(qt, k, vt, out_gain)


def _post_kernel(h_ref, hp_ref, hn_ref, oa_ref, oap_ref, oan_ref, oc_ref, ocp_ref, ocn_ref, p_ref,
                 wo_ref, fg_ref, wup_ref, fcw_ref, wdn_ref, wpg_ref, wpp_ref,
                 y_ref, hm_ref, a2_ref, ug_ref, uu_ref, acc_ref):
    i = pl.program_id(1)
    tm = h_ref.shape[0]
    lo, hi = F32_ROWS, F32_ROWS + tm

    def extended(main_ref, prev_ref, next_ref):
        prev = prev_ref[...].astype(F32)[BF16_ROWS - F32_ROWS:]
        nxt = next_ref[...].astype(F32)[:F32_ROWS]
        return jnp.concatenate([prev, main_ref[...].astype(F32), nxt], axis=0).astype(BF16)

    oa = extended(oa_ref, oap_ref, oan_ref)
    oc = extended(oc_ref, ocp_ref, ocn_ref)
    x = jnp.concatenate([hp_ref[...], h_ref[...], hn_ref[...]], axis=0)
    hm = (x + jnp.dot(oa, wo_ref[:ATTN_W, :], preferred_element_type=F32)
          + jnp.dot(oc, wo_ref[ATTN_W:, :], preferred_element_type=F32))
    hm_ref[...] = hm[lo:hi]
    keep = _edge_mask(hm.shape[0], tm, i == 0, i == pl.num_programs(1) - 1)
    a2_ref[...] = jnp.where(keep, _rms(hm, fg_ref[...]), 0.0).astype(BF16)
    acc_ref[...] = jnp.zeros_like(acc_ref)

    def up_proj(c, slot):
        a2 = a2_ref[...]
        ug_ref[slot] = jnp.dot(a2, wup_ref[0, c], preferred_element_type=F32)
        uu_ref[slot] = jnp.dot(a2, wup_ref[1, c], preferred_element_type=F32)

    def act_down(c, slot):
        gate = _dwconv3(ug_ref.at[slot], fcw_ref[0, c], 0, tm)
        lin = _dwconv3(uu_ref.at[slot], fcw_ref[1, c], 0, tm)
        act = (gate * jax.nn.sigmoid(gate) * lin).astype(BF16)
        acc_ref[...] += jnp.dot(act, wdn_ref[c], preferred_element_type=F32)

    up_proj(0, 0)

    def ff_pair(i, carry):
        c = 2 * i
        up_proj(c + 1, 1)
        act_down(c, 0)
        up_proj(c + 2, 0)
        act_down(c + 1, 1)
        return carry

    lax.fori_loop(0, (N_FF_CHUNKS - 1) // 2, ff_pair, 0)
    act_down(N_FF_CHUNKS - 1, 0)

    h2 = hm_ref[...] + acc_ref[...]
    gate = jax.nn.sigmoid(jnp.dot(h2.astype(BF16), wpg_ref[...], preferred_element_type=F32))
    emb = jnp.dot(p_ref[...].astype(BF16), wpp_ref[...], preferred_element_type=F32)
    y_ref[...] = h2 + gate * emb


def _post(h, o_attn, o_conv, p, lw):
    bsz, seq, _ = h.shape
    tm = min(TM_POST, seq)
    ext = tm + 2 * F32_ROWS
    return pl.pallas_call(
        _post_kernel,
        grid=(bsz, seq // tm),
        in_specs=(_halo_specs(tm, seq, D_MODEL, F32_ROWS)
                  + _halo_specs(tm, seq, ATTN_W, BF16_ROWS)
                  + _halo_specs(tm, seq, CONV_W, BF16_ROWS)
                  + [pl.BlockSpec((None, tm, PLE_DIM), lambda b, i: (b, i, 0)),
                     _resident((ATTN_W + CONV_W, D_MODEL)), _resident((1, D_MODEL)),
                     _resident((2, N_FF_CHUNKS, D_MODEL, FF_CHUNK)),
                     _resident((2, N_FF_CHUNKS, 3, FF_CHUNK)),
                     _resident((N_FF_CHUNKS, FF_CHUNK, D_MODEL)),
                     _resident((D_MODEL, D_MODEL)), _resident((PLE_DIM, D_MODEL))]),
        out_specs=pl.BlockSpec((None, tm, D_MODEL), lambda b, i: (b, i, 0)),
        out_shape=jax.ShapeDtypeStruct((bsz, seq, D_MODEL), F32),
        scratch_shapes=[pltpu.VMEM((tm, D_MODEL), F32),
                        pltpu.VMEM((ext, D_MODEL), BF16),
                        pltpu.VMEM((2, ext, FF_CHUNK), F32),
                        pltpu.VMEM((2, ext, FF_CHUNK), F32),                        pltpu.VMEM((tm, D_MODEL), F32)],
        compiler_params=pltpu.CompilerParams(
            dimension_semantics=("parallel", "arbitrary"), vmem_limit_bytes=VMEM_LIMIT_BYTES),
        name="post",
    )(h, h, h, o_attn, o_attn, o_attn, o_conv, o_conv, o_conv, p,
      lw["w_out"], lw["ffn_norm_g"], lw["w_up"], lw["ffn_conv_w"], lw["w_down"],
      lw["w_ple_gate"], lw["w_ple_proj"])


def _rope_tables(seq):
    rows = seq // GRID_W
    row = jnp.repeat(jnp.arange(rows), GRID_W).astype(F32)
    col = jnp.tile(jnp.arange(GRID_W), rows).astype(F32)
    half = HEAD_DIM // 2
    freqs = ROPE_THETA ** (-jnp.arange(0, half, 2, dtype=F32) / half)
    ang_r = row[:, None] * freqs[None, :]
    ang_c = col[:, None] * freqs[None, :]
    ang = jnp.concatenate([ang_r, ang_r, ang_c, ang_c], axis=-1)
    cos, sin = jnp.cos(ang), jnp.sin(ang)
    first = (jnp.arange(HEAD_DIM) % (HEAD_DIM // 2)) < HEAD_DIM // 4
    sin_up = jnp.where(first, -sin, 0.0)
    sin_dn = jnp.where(first, 0.0, sin)
    widen = lambda t: jnp.tile(t, (1, LANES // HEAD_DIM))
    return widen(cos), widen(sin_up), widen(sin_dn)


def _layer_weights(i, attn_norm_g, w_in, q_norm_g, k_norm_g, mix_conv_w, attn_out_g, conv_out_g,
                   w_out, ffn_norm_g, w_up, ffn_conv_w, w_down, w_ple_gate, w_ple_proj):
    head = jnp.arange(ATTN_W) // HEAD_DIM
    chunked = lambda w, lead: w.reshape(lead, 2, N_FF_CHUNKS, FF_CHUNK).transpose(1, 2, 0, 3)
    return {
        "attn_norm_g": attn_norm_g[i][None],
        "w_in": w_in[i].astype(BF16),
        "q_gain": jnp.tile(q_norm_g[i] * (math.log2(math.e) / math.sqrt(HEAD_DIM)), N_HEADS)[None],
        "k_gain": jnp.tile(k_norm_g[i], N_KV_HEADS)[None],
        "head_group": jnp.where(head[:, None] == head[None, :], 1.0 / HEAD_DIM, 0.0).astype(BF16),
        "mix_conv_w": mix_conv_w[i],
        "attn_out_g": attn_out_g[i][None],
        "conv_out_g": conv_out_g[i][None],
        "w_out": w_out[i].astype(BF16),
        "ffn_norm_g": ffn_norm_g[i][None],
        "w_up": chunked(w_up[i], D_MODEL).astype(BF16),
        "ffn_conv_w": chunked(ffn_conv_w[i], 3),
        "w_down": w_down[i].astype(BF16).reshape(N_FF_CHUNKS, FF_CHUNK, D_MODEL),
        "w_ple_gate": w_ple_gate[i].astype(BF16),
        "w_ple_proj": w_ple_proj[i].astype(BF16),
    }


def _trunk(x, p, layers):
    tables = _rope_tables(x.shape[1])
    h = x
    for i, lw in enumerate(layers):
        qt, k, vt, o_conv = _mixer_in(h, lw, tables)
        o_attn = _attention(qt, k, vt, lw["attn_out_g"])
        h = _post(h, o_attn, o_conv, p[i], lw)
    return h


def kernel(x_prompt, x_sample, p_prompt, p_sample, attn_norm_g, w_in, q_norm_g, k_norm_g, mix_conv_w,
           attn_out_g, conv_out_g, w_out, ffn_norm_g, w_up, ffn_conv_w, w_down, w_ple_gate, w_ple_proj):
    depth = w_in.shape[0]
    layers = [_layer_weights(i, attn_norm_g, w_in, q_norm_g, k_norm_g, mix_conv_w, attn_out_g,
                             conv_out_g, w_out, ffn_norm_g, w_up, ffn_conv_w, w_down, w_ple_gate,
                             w_ple_proj) for i in range(depth)]
    return (_trunk(x_prompt, p_prompt, layers), _trunk(x_sample, p_sample, layers))
```

```python
import functools
import math

import jax
import jax.numpy as jnp
from jax import lax
from jax.experimental import pallas as pl
from jax.experimental.pallas import tpu as pltpu

D_MODEL = 1024
GRID_W = 64
HEAD_DIM = 64
N_HEADS = 8
N_KV_HEADS = 2
Q_PER_KV = N_HEADS // N_KV_HEADS
ATTN_W = N_HEADS * HEAD_DIM
KV_W = N_KV_HEADS * HEAD_DIM
CONV_W = 512
D_IN = ATTN_W + 2 * KV_W + 3 * CONV_W
D_FF = 2816
PLE_DIM = 256
ROPE_THETA = 10000.0
EPS = 1e-6

LANES = 128
F32_ROWS = 8
BF16_ROWS = 16
VMEM_LIMIT_BYTES = 56 * 1024 * 1024

TM_IN = 512
TQ = 256
TK = 512
SCORE_LOOKAHEAD = 2
V_ROWS = HEAD_DIM + BF16_ROWS
TM_POST = 512
FF_CHUNK = 256
N_FF_CHUNKS = D_FF // FF_CHUNK
assert N_FF_CHUNKS * FF_CHUNK == D_FF and N_FF_CHUNKS % 2 == 1

F32 = jnp.float32
BF16 = jnp.bfloat16


def _resident(shape):
    zeros = (0,) * len(shape)
    return pl.BlockSpec(shape, lambda *_: zeros, pipeline_mode=pl.Buffered(1))


def _rms(x, g):
    return x * lax.rsqrt(jnp.mean(x * x, axis=-1, keepdims=True) + EPS) * g


def _rope(x, cos, sin_up, sin_dn):
    cols = []
    for c in range(x.shape[1] // LANES):
        xc = x[:, c * LANES:(c + 1) * LANES]
        up = pltpu.roll(xc, LANES - HEAD_DIM // 4, 1)
        dn = pltpu.roll(xc, HEAD_DIM // 4, 1)
        cols.append(xc * cos + up * sin_up + dn * sin_dn)
    return jnp.concatenate(cols, axis=1) if len(cols) > 1 else cols[0]


def _edge_mask(rows, tm, is_first, is_last):
    r = lax.broadcasted_iota(jnp.int32, (rows, 1), 0)
    outside = jnp.logical_or(jnp.logical_and(r < F32_ROWS, is_first),
                             jnp.logical_and(r >= F32_ROWS + tm, is_last))
    return jnp.logical_not(outside)


def _dwconv3(ref, w, r0, nrows):
    x = ref[r0:r0 + nrows + 2 * F32_ROWS, :]
    lo = F32_ROWS
    return (pltpu.roll(x, 1, 0)[lo:lo + nrows] * w[0:1, :] + x[lo:lo + nrows] * w[1:2, :]
            + pltpu.roll(x, x.shape[0] - 1, 0)[lo:lo + nrows] * w[2:3, :])


def _mixer_in_kernel(h_ref, hp_ref, hn_ref, ng_ref, w_ref, qg_ref, kg_ref, cos_ref, su_ref, sd_ref,
                     grp_ref, cw_ref, cg_ref,
                     qt_ref, k_ref, vt_ref, oc_ref, cu_ref):
    i = pl.program_id(1)
    tm = h_ref.shape[0]
    lo, hi = F32_ROWS, F32_ROWS + tm
    x = jnp.concatenate([hp_ref[...], h_ref[...], hn_ref[...]], axis=0)
    a = _rms(x, ng_ref[...]).astype(BF16)

    def proj(c0, c1):
        return jnp.dot(a, w_ref[:, c0:c1], preferred_element_type=F32)

    cos, s_up, s_dn = cos_ref[...], su_ref[...], sd_ref[...]

    def head_norm(z, g):
        w = z.shape[1]
        ms = jnp.dot((z * z).astype(BF16), grp_ref[:w, :w], preferred_element_type=F32)
        return z * lax.rsqrt(ms + EPS) * g

    c0 = ATTN_W + 2 * KV_W
    cu = proj(c0 + CONV_W, c0 + 2 * CONV_W) * proj(c0 + 2 * CONV_W, c0 + 3 * CONV_W)
    keep = _edge_mask(cu.shape[0], tm, i == 0, i == pl.num_programs(1) - 1)
    cu_ref[...] = jnp.where(keep, cu, 0.0)

    zq = proj(0, ATTN_W)[lo:hi]
    conv = _dwconv3(cu_ref, cw_ref[...], 0, tm)
    qt_ref[...] = _rope(head_norm(zq, qg_ref[...]), cos, s_up, s_dn).T.astype(BF16)

    zkv = proj(ATTN_W, ATTN_W + 2 * KV_W)[lo:hi]
    k = _rope(head_norm(zkv[:, :KV_W], kg_ref[...]), cos, s_up, s_dn).astype(BF16)
    for g in range(N_KV_HEADS):
        k_ref[g] = k[:, g * HEAD_DIM:(g + 1) * HEAD_DIM]
    vt = zkv[:, KV_W:].T.astype(BF16)
    for g in range(N_KV_HEADS):
        vt_ref[g * V_ROWS:g * V_ROWS + HEAD_DIM, :] = vt[g * HEAD_DIM:(g + 1) * HEAD_DIM]
        vt_ref[g * V_ROWS + HEAD_DIM:(g + 1) * V_ROWS, :] = jnp.ones((BF16_ROWS, tm), BF16)

    gb = proj(c0, c0 + CONV_W)[lo:hi]
    oc_ref[...] = _rms(gb * conv, cg_ref[...]).astype(BF16)


def _halo_specs(tm, seq, width, rows):
    per = tm // rows
    last = seq // rows - 1
    return [
        pl.BlockSpec((None, tm, width), lambda b, i: (b, i, 0)),
        pl.BlockSpec((None, rows, width), lambda b, i: (b, jnp.maximum(i * per - 1, 0), 0)),
        pl.BlockSpec((None, rows, width), lambda b, i: (b, jnp.minimum((i + 1) * per, last), 0)),
    ]


def _mixer_in(h, lw, tables):
    bsz, seq, _ = h.shape
    tm = min(TM_IN, seq)
    cos, s_up, s_dn = tables
    tab_spec = pl.BlockSpec((tm, LANES), lambda b, i: (i, 0))
    tok = lambda w: pl.BlockSpec((None, tm, w), lambda b, i: (b, i, 0))
    tok_t = lambda w: pl.BlockSpec((None, w, tm), lambda b, i: (b, 0, i))
    return pl.pallas_call(
        _mixer_in_kernel,
        grid=(bsz, seq // tm),
        in_specs=_halo_specs(tm, seq, D_MODEL, F32_ROWS) + [
            _resident((1, D_MODEL)), _resident((D_MODEL, D_IN)),
            _resident((1, ATTN_W)), _resident((1, KV_W)),
            tab_spec, tab_spec, tab_spec,
            _resident((ATTN_W, ATTN_W)), _resident((3, CONV_W)), _resident((1, CONV_W)),
        ],
        out_specs=[tok_t(ATTN_W),
                   pl.BlockSpec((None, N_KV_HEADS, tm, HEAD_DIM), lambda b, i: (b, 0, i, 0)),
                   tok_t(N_KV_HEADS * V_ROWS), tok(CONV_W)],
        out_shape=[jax.ShapeDtypeStruct((bsz, ATTN_W, seq), BF16),
                   jax.ShapeDtypeStruct((bsz, N_KV_HEADS, seq, HEAD_DIM), BF16),
                   jax.ShapeDtypeStruct((bsz, N_KV_HEADS * V_ROWS, seq), BF16),
                   jax.ShapeDtypeStruct((bsz, seq, CONV_W), BF16)],
        scratch_shapes=[pltpu.VMEM((tm + 2 * F32_ROWS, CONV_W), F32)],
        compiler_params=pltpu.CompilerParams(
            dimension_semantics=("parallel", "arbitrary"), vmem_limit_bytes=VMEM_LIMIT_BYTES),
        name="mixer_in",
    )(h, h, h, lw["attn_norm_g"], lw["w_in"], lw["q_gain"], lw["k_gain"], cos, s_up, s_dn,
      lw["head_group"], lw["mix_conv_w"], lw["conv_out_g"])


def _attention_kernel(qt_ref, k_ref, vt_ref, og_ref, o_ref, m_ref, acc_ref, ot_ref, stage_ref):
    seq = k_ref.shape[1]
    n_chunks = seq // TK
    half = TK // 2

    m_ref[...] = jnp.full_like(m_ref, -jnp.inf)
    acc_ref[...] = jnp.zeros_like(acc_ref)

    def stage_scores(c, h):
        start = pl.multiple_of(c * TK, TK)
        for j in range(2):
            stage_ref[h, j] = jnp.dot(k_ref[h // Q_PER_KV, pl.ds(start + j * half, half), :],
                                      qt_ref[h * HEAD_DIM:(h + 1) * HEAD_DIM, :],
                                      preferred_element_type=F32)

    for h in range(SCORE_LOOKAHEAD):
        stage_scores(0, h)

    def chunk(c, carry):
        start = pl.multiple_of(c * TK, TK)
        c_next = jnp.minimum(c + 1, n_chunks - 1)
        for h in range(N_HEADS):
            g = h // Q_PER_KV
            ahead = h + SCORE_LOOKAHEAD
            if ahead < N_HEADS:
                stage_scores(c, ahead)
            else:
                stage_scores(c_next, ahead - N_HEADS)
            m_old = m_ref[h]
            m_new = jnp.maximum(m_old, jnp.maximum(jnp.max(stage_ref[h, 0], axis=0, keepdims=True),
                                                   jnp.max(stage_ref[h, 1], axis=0, keepdims=True)))
            alpha = jnp.exp2(m_old - m_new)
            pb = jnp.concatenate([jnp.exp2(stage_ref[h, j] - m_new).astype(BF16) for j in range(2)],
                                 axis=0)
            vt = vt_ref[g * V_ROWS:(g + 1) * V_ROWS, pl.ds(start, TK)]
            acc_ref[h] = alpha * acc_ref[h] + jnp.dot(vt, pb, preferred_element_type=F32)
            m_ref[h] = m_new
        return carry

    lax.fori_loop(0, n_chunks, chunk, 0)

    for h in range(N_HEADS):
        ot_ref[h * HEAD_DIM:(h + 1) * HEAD_DIM, :] = (
            acc_ref[h, :HEAD_DIM, :] / acc_ref[h, HEAD_DIM:HEAD_DIM + 1, :])
    o_ref[...] = _rms(ot_ref[...].T, og_ref[...]).astype(BF16)


def _attention(qt, k, vt, out_gain):
    bsz, _, seq, _ = k.shape
    tq = min(TQ, seq)
    return pl.pallas_call(
        _attention_kernel,
        grid=(bsz, seq // tq),
        in_specs=[pl.BlockSpec((None, ATTN_W, tq), lambda b, i: (b, 0, i)),
                  pl.BlockSpec((None, N_KV_HEADS, seq, HEAD_DIM), lambda b, i: (b, 0, 0, 0)),
                  pl.BlockSpec((None, N_KV_HEADS * V_ROWS, seq), lambda b, i: (b, 0, 0)),
                  _resident((1, ATTN_W))],
        out_specs=pl.BlockSpec((None, tq, ATTN_W), lambda b, i: (b, i, 0)),
        out_shape=jax.ShapeDtypeStruct((bsz, seq, ATTN_W), BF16),
        scratch_shapes=[pltpu.VMEM((N_HEADS, 1, tq), F32),
                        pltpu.VMEM((N_HEADS, V_ROWS, tq), F32),
                        pltpu.VMEM((ATTN_W, tq), F32),
                        pltpu.VMEM((N_HEADS, 2, TK // 2, tq), F32)],
        compiler_params=pltpu.CompilerParams(
            dimension_semantics=("parallel", "arbitrary"), vmem_limit_bytes=VMEM_LIMIT_BYTES),
        name="attention",
    )(qt, k, vt, out_gain)


def _post_kernel(h_ref, hp_ref, hn_ref, oa_ref, oap_ref, oan_ref, oc_ref, ocp_ref, ocn_ref, p_ref,
                 wo_ref, fg_ref, wup_ref, fcw_ref, wdn_ref, wpg_ref, wpp_ref,
                 y_ref, hm_ref, a2_ref, ug_ref, uu_ref, acc_ref):
    i = pl.program_id(1)
    tm = h_ref.shape[0]
    lo, hi = F32_ROWS, F32_ROWS + tm

    def extended(main_ref, prev_ref, next_ref):
        prev = prev_ref[...].astype(F32)[BF16_ROWS - F32_ROWS:]
        nxt = next_ref[...].astype(F32)[:F32_ROWS]
        return jnp.concatenate([prev, main_ref[...].astype(F32), nxt], axis=0).astype(BF16)

    oa = extended(oa_ref, oap_ref, oan_ref)
    oc = extended(oc_ref, ocp_ref, ocn_ref)
    x = jnp.concatenate([hp_ref[...], h_ref[...], hn_ref[...]], axis=0)
    hm = (x + jnp.dot(oa, wo_ref[:ATTN_W, :], preferred_element_type=F32)
          + jnp.dot(oc, wo_ref[ATTN_W:, :], preferred_element_type=F32))
    hm_ref[...] = hm[lo:hi]
    keep = _edge_mask(hm.shape[0], tm, i == 0, i == pl.num_programs(1) - 1)
    a2_ref[...] = jnp.where(keep, _rms(hm, fg_ref[...]), 0.0).astype(BF16)
    acc_ref[...] = jnp.zeros_like(acc_ref)

    def ff_cols(which, c):
        return pl.ds(pl.multiple_of(which * D_FF + c * FF_CHUNK, FF_CHUNK), FF_CHUNK)

    def up_proj(c, slot):
        a2 = a2_ref[...]
        ug_ref[slot] = jnp.dot(a2, wup_ref[:, ff_cols(0, c)], preferred_element_type=F32)
        uu_ref[slot] = jnp.dot(a2, wup_ref[:, ff_cols(1, c)], preferred_element_type=F32)

    def act_down(c, slot):
        gate = _dwconv3(ug_ref.at[slot], fcw_ref[:, ff_cols(0, c)], 0, tm)
        lin = _dwconv3(uu_ref.at[slot], fcw_ref[:, ff_cols(1, c)], 0, tm)
        act = (gate * jax.nn.sigmoid(gate) * lin).astype(BF16)
        acc_ref[...] += jnp.dot(act, wdn_ref[c], preferred_element_type=F32)

    up_proj(0, 0)

    def ff_pair(i, carry):
        c = 2 * i
        up_proj(c + 1, 1)
        act_down(c, 0)
        up_proj(c + 2, 0)
        act_down(c + 1, 1)
        return carry

    lax.fori_loop(0, (N_FF_CHUNKS - 1) // 2, ff_pair, 0)
    act_down(N_FF_CHUNKS - 1, 0)

    h2 = hm_ref[...] + acc_ref[...]
    gate = jax.nn.sigmoid(jnp.dot(h2.astype(BF16), wpg_ref[...], preferred_element_type=F32))
    emb = jnp.dot(p_ref[...].astype(BF16), wpp_ref[...], preferred_element_type=F32)
    y_ref[...] = h2 + gate * emb


def _post(h, o_attn, o_conv, p, layer, lw):
    bsz, seq, _ = h.shape
    tm = min(TM_POST, seq)
    ext = tm + 2 * F32_ROWS
    return pl.pallas_call(
        _post_kernel,
        grid=(bsz, seq // tm),
        in_specs=(_halo_specs(tm, seq, D_MODEL, F32_ROWS)
                  + _halo_specs(tm, seq, ATTN_W, BF16_ROWS)
                  + _halo_specs(tm, seq, CONV_W, BF16_ROWS)
                  + [pl.BlockSpec((None, None, tm, PLE_DIM), lambda b, i: (layer, b, i, 0)),
                     _resident((ATTN_W + CONV_W, D_MODEL)), _resident((1, D_MODEL)),
                     _resident((D_MODEL, 2 * D_FF)),
                     _resident((3, 2 * D_FF)),
                     _resident((N_FF_CHUNKS, FF_CHUNK, D_MODEL)),
                     _resident((D_MODEL, D_MODEL)), _resident((PLE_DIM, D_MODEL))]),
        out_specs=pl.BlockSpec((None, tm, D_MODEL), lambda b, i: (b, i, 0)),
        out_shape=jax.ShapeDtypeStruct((bsz, seq, D_MODEL), F32),
        scratch_shapes=[pltpu.VMEM((tm, D_MODEL), F32),
                        pltpu.VMEM((ext, D_MODEL), BF16),
                        pltpu.VMEM((2, ext, FF_CHUNK), F32),
                        pltpu.VMEM((2, ext, FF_CHUNK), F32),                        pltpu.VMEM((tm, D_MODEL), F32)],
        compiler_params=pltpu.CompilerParams(
            dimension_semantics=("parallel", "arbitrary"), vmem_limit_bytes=VMEM_LIMIT_BYTES),
        name="post",
    )(h, h, h, o_attn, o_attn, o_attn, o_conv, o_conv, o_conv, p,
      lw["w_out"], lw["ffn_norm_g"], lw["w_up"], lw["ffn_conv_w"], lw["w_down"],
      lw["w_ple_gate"], lw["w_ple_proj"])


def _rope_tables(seq):
    rows = seq // GRID_W
    row = jnp.repeat(jnp.arange(rows), GRID_W).astype(F32)
    col = jnp.tile(jnp.arange(GRID_W), rows).astype(F32)
    half = HEAD_DIM // 2
    freqs = ROPE_THETA ** (-jnp.arange(0, half, 2, dtype=F32) / half)
    ang_r = row[:, None] * freqs[None, :]
    ang_c = col[:, None] * freqs[None, :]
    ang = jnp.concatenate([ang_r, ang_r, ang_c, ang_c], axis=-1)
    cos, sin = jnp.cos(ang), jnp.sin(ang)
    first = (jnp.arange(HEAD_DIM) % (HEAD_DIM // 2)) < HEAD_DIM // 4
    sin_up = jnp.where(first, -sin, 0.0)
    sin_dn = jnp.where(first, 0.0, sin)
    widen = lambda t: jnp.tile(t, (1, LANES // HEAD_DIM))
    return widen(cos), widen(sin_up), widen(sin_dn)


def _layer_weights(i, attn_norm_g, w_in, q_norm_g, k_norm_g, mix_conv_w, attn_out_g, conv_out_g,
                   w_out, ffn_norm_g, w_up, ffn_conv_w, w_down, w_ple_gate, w_ple_proj):
    head = jnp.arange(ATTN_W) // HEAD_DIM
    return {
        "attn_norm_g": attn_norm_g[i][None],
        "w_in": w_in[i].astype(BF16),
        "q_gain": jnp.tile(q_norm_g[i] * (math.log2(math.e) / math.sqrt(HEAD_DIM)), N_HEADS)[None],
        "k_gain": jnp.tile(k_norm_g[i], N_KV_HEADS)[None],
        "head_group": jnp.where(head[:, None] == head[None, :], 1.0 / HEAD_DIM, 0.0).astype(BF16),
        "mix_conv_w": mix_conv_w[i],
        "attn_out_g": attn_out_g[i][None],
        "conv_out_g": conv_out_g[i][None],
        "w_out": w_out[i].astype(BF16),
        "ffn_norm_g": ffn_norm_g[i][None],
        "w_up": w_up[i].astype(BF16),
        "ffn_conv_w": ffn_conv_w[i],
        "w_down": w_down[i].astype(BF16).reshape(N_FF_CHUNKS, FF_CHUNK, D_MODEL),
        "w_ple_gate": w_ple_gate[i].astype(BF16),
        "w_ple_proj": w_ple_proj[i].astype(BF16),
    }


def _trunk(x, p, layers):
    tables = _rope_tables(x.shape[1])
    h = x
    for i, lw in enumerate(layers):
        qt, k, vt, o_conv = _mixer_in(h, lw, tables)
        o_attn = _attention(qt, k, vt, lw["attn_out_g"])
        h = _post(h, o_attn, o_conv, p, i, lw)
    return h


def kernel(x_prompt, x_sample, p_prompt, p_sample, attn_norm_g, w_in, q_norm_g, k_norm_g, mix_conv_w,
           attn_out_g, conv_out_g, w_out, ffn_norm_g, w_up, ffn_conv_w, w_down, w_ple_gate, w_ple_proj):
    depth = w_in.shape[0]
    layers = [_layer_weights(i, attn_norm_g, w_in, q_norm_g, k_norm_g, mix_conv_w, attn_out_g,
                             conv_out_g, w_out, ffn_norm_g, w_up, ffn_conv_w, w_down, w_ple_gate,
                             w_ple_proj) for i in range(depth)]
    return (_trunk(x_prompt, p_prompt, layers), _trunk(x_sample, p_sample, layers))
```

```python
import functools
import math

import jax
import jax.numpy as jnp
from jax import lax
from jax.experimental import pallas as pl
from jax.experimental.pallas import tpu as pltpu

D_MODEL = 1024
GRID_W = 64
HEAD_DIM = 64
N_HEADS = 8
N_KV_HEADS = 2
Q_PER_KV = N_HEADS // N_KV_HEADS
ATTN_W = N_HEADS * HEAD_DIM
KV_W = N_KV_HEADS * HEAD_DIM
CONV_W = 512
D_IN = ATTN_W + 2 * KV_W + 3 * CONV_W
D_FF = 2816
PLE_DIM = 256
ROPE_THETA = 10000.0
EPS = 1e-6

LANES = 128
F32_ROWS = 8
BF16_ROWS = 16
VMEM_LIMIT_BYTES = 56 * 1024 * 1024

TM_IN = 1024
TQ = 256
N_KEY_CHUNKS = 4
SCORE_LOOKAHEAD = 2
V_ROWS = HEAD_DIM + BF16_ROWS
TM_POST = 512
FF_CHUNK = 256
N_FF_CHUNKS = D_FF // FF_CHUNK
assert N_FF_CHUNKS * FF_CHUNK == D_FF and N_FF_CHUNKS % 2 == 1

F32 = jnp.float32
BF16 = jnp.bfloat16


def _resident(shape):
    zeros = (0,) * len(shape)
    return pl.BlockSpec(shape, lambda *_: zeros, pipeline_mode=pl.Buffered(1))


def _rms(x, g):
    return x * lax.rsqrt(jnp.mean(x * x, axis=-1, keepdims=True) + EPS) * g


def _rope(x, cos, sin_up, sin_dn):
    cols = []
    for c in range(x.shape[1] // LANES):
        xc = x[:, c * LANES:(c + 1) * LANES]
        up = pltpu.roll(xc, LANES - HEAD_DIM // 4, 1)
        dn = pltpu.roll(xc, HEAD_DIM // 4, 1)
        cols.append(xc * cos + up * sin_up + dn * sin_dn)
    return jnp.concatenate(cols, axis=1) if len(cols) > 1 else cols[0]


def _edge_mask(rows, tm, is_first, is_last):
    r = lax.broadcasted_iota(jnp.int32, (rows, 1), 0)
    outside = jnp.logical_or(jnp.logical_and(r < F32_ROWS, is_first),
                             jnp.logical_and(r >= F32_ROWS + tm, is_last))
    return jnp.logical_not(outside)


def _dwconv3(ref, w, r0, nrows):
    x = ref[r0:r0 + nrows + 2 * F32_ROWS, :]
    lo = F32_ROWS
    return (pltpu.roll(x, 1, 0)[lo:lo + nrows] * w[0:1, :] + x[lo:lo + nrows] * w[1:2, :]
            + pltpu.roll(x, x.shape[0] - 1, 0)[lo:lo + nrows] * w[2:3, :])


def _mixer_in_kernel(h_ref, hp_ref, hn_ref, ng_ref, w_ref, qg_ref, kg_ref, cos_ref, su_ref, sd_ref,
                     grp_ref, cw_ref, cg_ref,
                     qt_ref, k_ref, vt_ref, oc_ref, cu_ref):
    i = pl.program_id(1)
    tm = h_ref.shape[0]
    lo, hi = F32_ROWS, F32_ROWS + tm
    x = jnp.concatenate([hp_ref[...], h_ref[...], hn_ref[...]], axis=0)
    a = _rms(x, ng_ref[...]).astype(BF16)

    def proj(c0, c1):
        return jnp.dot(a, w_ref[:, c0:c1], preferred_element_type=F32)

    cos, s_up, s_dn = cos_ref[...], su_ref[...], sd_ref[...]

    def head_norm(z, g):
        w = z.shape[1]
        ms = jnp.dot((z * z).astype(BF16), grp_ref[:w, :w], preferred_element_type=F32)
        return z * lax.rsqrt(ms + EPS) * g

    c0 = ATTN_W + 2 * KV_W
    cu = proj(c0 + CONV_W, c0 + 2 * CONV_W) * proj(c0 + 2 * CONV_W, c0 + 3 * CONV_W)
    keep = _edge_mask(cu.shape[0], tm, i == 0, i == pl.num_programs(1) - 1)
    cu_ref[...] = jnp.where(keep, cu, 0.0)

    zq = proj(0, ATTN_W)[lo:hi]
    conv = _dwconv3(cu_ref, cw_ref[...], 0, tm)
    qt_ref[...] = _rope(head_norm(zq, qg_ref[...]), cos, s_up, s_dn).T.astype(BF16)

    zkv = proj(ATTN_W, ATTN_W + 2 * KV_W)[lo:hi]
    k = _rope(head_norm(zkv[:, :KV_W], kg_ref[...]), cos, s_up, s_dn).astype(BF16)
    for g in range(N_KV_HEADS):
        k_ref[g] = k[:, g * HEAD_DIM:(g + 1) * HEAD_DIM]
    vt = zkv[:, KV_W:].T.astype(BF16)
    for g in range(N_KV_HEADS):
        vt_ref[g * V_ROWS:g * V_ROWS + HEAD_DIM, :] = vt[g * HEAD_DIM:(g + 1) * HEAD_DIM]
        vt_ref[g * V_ROWS + HEAD_DIM:(g + 1) * V_ROWS, :] = jnp.ones((BF16_ROWS, tm), BF16)

    gb = proj(c0, c0 + CONV_W)[lo:hi]
    oc_ref[...] = _rms(gb * conv, cg_ref[...]).astype(BF16)


def _halo_specs(tm, seq, width, rows):
    per = tm // rows
    last = seq // rows - 1
    return [
        pl.BlockSpec((None, tm, width), lambda b, i: (b, i, 0)),
        pl.BlockSpec((None, rows, width), lambda b, i: (b, jnp.maximum(i * per - 1, 0), 0)),
        pl.BlockSpec((None, rows, width), lambda b, i: (b, jnp.minimum((i + 1) * per, last), 0)),
    ]


def _mixer_in(h, lw, tables):
    bsz, seq, _ = h.shape
    tm = min(TM_IN, seq)
    cos, s_up, s_dn = tables
    tab_spec = pl.BlockSpec((tm, LANES), lambda b, i: (i, 0))
    tok = lambda w: pl.BlockSpec((None, tm, w), lambda b, i: (b, i, 0))
    tok_t = lambda w: pl.BlockSpec((None, w, tm), lambda b, i: (b, 0, i))
    return pl.pallas_call(
        _mixer_in_kernel,
        grid=(bsz, seq // tm),
        in_specs=_halo_specs(tm, seq, D_MODEL, F32_ROWS) + [
            _resident((1, D_MODEL)), _resident((D_MODEL, D_IN)),
            _resident((1, ATTN_W)), _resident((1, KV_W)),
            tab_spec, tab_spec, tab_spec,
            _resident((ATTN_W, ATTN_W)), _resident((3, CONV_W)), _resident((1, CONV_W)),
        ],
        out_specs=[tok_t(ATTN_W),
                   pl.BlockSpec((None, N_KV_HEADS, tm, HEAD_DIM), lambda b, i: (b, 0, i, 0)),
                   tok_t(N_KV_HEADS * V_ROWS), tok(CONV_W)],
        out_shape=[jax.ShapeDtypeStruct((bsz, ATTN_W, seq), BF16),
                   jax.ShapeDtypeStruct((bsz, N_KV_HEADS, seq, HEAD_DIM), BF16),
                   jax.ShapeDtypeStruct((bsz, N_KV_HEADS * V_ROWS, seq), BF16),
                   jax.ShapeDtypeStruct((bsz, seq, CONV_W), BF16)],
        scratch_shapes=[pltpu.VMEM((tm + 2 * F32_ROWS, CONV_W), F32)],
        compiler_params=pltpu.CompilerParams(
            dimension_semantics=("parallel", "arbitrary"), vmem_limit_bytes=VMEM_LIMIT_BYTES),
        name="mixer_in",
    )(h, h, h, lw["attn_norm_g"], lw["w_in"], lw["q_gain"], lw["k_gain"], cos, s_up, s_dn,
      lw["head_group"], lw["mix_conv_w"], lw["conv_out_g"])


def _attention_kernel(qt_ref, k_ref, vt_ref, og_ref, o_ref, m_ref, acc_ref, ot_ref, stage_ref):
    seq = k_ref.shape[1]
    n_chunks = N_KEY_CHUNKS
    tk = seq // n_chunks
    half = tk // 2

    m_ref[...] = jnp.full_like(m_ref, -jnp.inf)
    acc_ref[...] = jnp.zeros_like(acc_ref)

    def stage_scores(c, h):
        start = pl.multiple_of(c * tk, tk)
        for j in range(2):
            stage_ref[h, j] = jnp.dot(k_ref[h // Q_PER_KV, pl.ds(start + j * half, half), :],
                                      qt_ref[h * HEAD_DIM:(h + 1) * HEAD_DIM, :],
                                      preferred_element_type=F32)

    for h in range(SCORE_LOOKAHEAD):
        stage_scores(0, h)

    def chunk(c, wrap):
        start = pl.multiple_of(c * tk, tk)
        for h in range(N_HEADS):
            g = h // Q_PER_KV
            ahead = h + SCORE_LOOKAHEAD
            if ahead < N_HEADS:
                stage_scores(c, ahead)
            elif wrap:
                stage_scores(c + 1, ahead - N_HEADS)
            m_old = m_ref[h]
            m_new = jnp.maximum(m_old, jnp.maximum(jnp.max(stage_ref[h, 0], axis=0, keepdims=True),
                                                   jnp.max(stage_ref[h, 1], axis=0, keepdims=True)))
            alpha = jnp.exp2(m_old - m_new)
            pb = jnp.concatenate([jnp.exp2(stage_ref[h, j] - m_new).astype(BF16) for j in range(2)],
                                 axis=0)
            vt = vt_ref[g * V_ROWS:(g + 1) * V_ROWS, pl.ds(start, tk)]
            acc_ref[h] = alpha * acc_ref[h] + jnp.dot(vt, pb, preferred_element_type=F32)
            m_ref[h] = m_new

    def wrapped(c, carry):
        chunk(c, True)
        return carry

    lax.fori_loop(0, n_chunks - 1, wrapped, 0)
    chunk(n_chunks - 1, False)

    for h in range(N_HEADS):
        ot_ref[h * HEAD_DIM:(h + 1) * HEAD_DIM, :] = (
            acc_ref[h, :HEAD_DIM, :] / acc_ref[h, HEAD_DIM:HEAD_DIM + 1, :])
    o_ref[...] = _rms(ot_ref[...].T, og_ref[...]).astype(BF16)


def _attention(qt, k, vt, out_gain):
    bsz, _, seq, _ = k.shape
    tq = min(TQ, seq)
    return pl.pallas_call(
        _attention_kernel,
        grid=(bsz, seq // tq),
        in_specs=[pl.BlockSpec((None, ATTN_W, tq), lambda b, i: (b, 0, i)),
                  pl.BlockSpec((None, N_KV_HEADS, seq, HEAD_DIM), lambda b, i: (b, 0, 0, 0)),
                  pl.BlockSpec((None, N_KV_HEADS * V_ROWS, seq), lambda b, i: (b, 0, 0)),
                  _resident((1, ATTN_W))],
        out_specs=pl.BlockSpec((None, tq, ATTN_W), lambda b, i: (b, i, 0)),
        out_shape=jax.ShapeDtypeStruct((bsz, seq, ATTN_W), BF16),
        scratch_shapes=[pltpu.VMEM((N_HEADS, 1, tq), F32),
                        pltpu.VMEM((N_HEADS, V_ROWS, tq), F32),
                        pltpu.VMEM((ATTN_W, tq), F32),
                        pltpu.VMEM((N_HEADS, 2, seq // N_KEY_CHUNKS // 2, tq), F32)],
        compiler_params=pltpu.CompilerParams(
            dimension_semantics=("parallel", "arbitrary"), vmem_limit_bytes=VMEM_LIMIT_BYTES),
        name="attention",
    )(qt, k, vt, out_gain)


def _post_kernel(h_ref, hp_ref, hn_ref, oa_ref, oap_ref, oan_ref, oc_ref, ocp_ref, ocn_ref, p_ref,
                 wo_ref, fg_ref, wup_ref, fcw_ref, wdn_ref, wpg_ref, wpp_ref,
                 y_ref, hm_ref, a2_ref, ug_ref, uu_ref, acc_ref):
    i = pl.program_id(1)
    tm = h_ref.shape[0]
    lo, hi = F32_ROWS, F32_ROWS + tm

    def extended(main_ref, prev_ref, next_ref):
        prev = prev_ref[...].astype(F32)[BF16_ROWS - F32_ROWS:]
        nxt = next_ref[...].astype(F32)[:F32_ROWS]
        return jnp.concatenate([prev, main_ref[...].astype(F32), nxt], axis=0).astype(BF16)

    oa = extended(oa_ref, oap_ref, oan_ref)
    oc = extended(oc_ref, ocp_ref, ocn_ref)
    x = jnp.concatenate([hp_ref[...], h_ref[...], hn_ref[...]], axis=0)
    hm = (x + jnp.dot(oa, wo_ref[:ATTN_W, :], preferred_element_type=F32)
          + jnp.dot(oc, wo_ref[ATTN_W:, :], preferred_element_type=F32))
    hm_ref[...] = hm[lo:hi]
    keep = _edge_mask(hm.shape[0], tm, i == 0, i == pl.num_programs(1) - 1)
    a2_ref[...] = jnp.where(keep, _rms(hm, fg_ref[...]), 0.0).astype(BF16)
    acc_ref[...] = jnp.zeros_like(acc_ref)

    def ff_cols(which, c):
        return pl.ds(pl.multiple_of(which * D_FF + c * FF_CHUNK, FF_CHUNK), FF_CHUNK)

    def up_proj(c, slot):
        a2 = a2_ref[...]
        ug_ref[slot] = jnp.dot(a2, wup_ref[:, ff_cols(0, c)], preferred_element_type=F32)
        uu_ref[slot] = jnp.dot(a2, wup_ref[:, ff_cols(1, c)], preferred_element_type=F32)

    def act_down(c, slot):
        gate = _dwconv3(ug_ref.at[slot], fcw_ref[:, ff_cols(0, c)], 0, tm)
        lin = _dwconv3(uu_ref.at[slot], fcw_ref[:, ff_cols(1, c)], 0, tm)
        act = (gate * jax.nn.sigmoid(gate) * lin).astype(BF16)
        acc_ref[...] += jnp.dot(act, wdn_ref[c], preferred_element_type=F32)

    up_proj(0, 0)

    def ff_pair(i, carry):
        c = 2 * i
        up_proj(c + 1, 1)
        act_down(c, 0)
        up_proj(c + 2, 0)
        act_down(c + 1, 1)
        return carry

    lax.fori_loop(0, (N_FF_CHUNKS - 1) // 2, ff_pair, 0)
    act_down(N_FF_CHUNKS - 1, 0)

    h2 = hm_ref[...] + acc_ref[...]
    gate = jax.nn.sigmoid(jnp.dot(h2.astype(BF16), wpg_ref[...], preferred_element_type=F32))
    emb = jnp.dot(p_ref[...].astype(BF16), wpp_ref[...], preferred_element_type=F32)
    y_ref[...] = h2 + gate * emb


def _post(h, o_attn, o_conv, p, layer, lw):
    bsz, seq, _ = h.shape
    tm = min(TM_POST, seq)
    ext = tm + 2 * F32_ROWS
    return pl.pallas_call(
        _post_kernel,
        grid=(bsz, seq // tm),
        in_specs=(_halo_specs(tm, seq, D_MODEL, F32_ROWS)
                  + _halo_specs(tm, seq, ATTN_W, BF16_ROWS)
                  + _halo_specs(tm, seq, CONV_W, BF16_ROWS)
                  + [pl.BlockSpec((None, None, tm, PLE_DIM), lambda b, i: (layer, b, i, 0)),
                     _resident((ATTN_W + CONV_W, D_MODEL)), _resident((1, D_MODEL)),
                     _resident((D_MODEL, 2 * D_FF)),
                     _resident((3, 2 * D_FF)),
                     _resident((N_FF_CHUNKS, FF_CHUNK, D_MODEL)),
                     _resident((D_MODEL, D_MODEL)), _resident((PLE_DIM, D_MODEL))]),
        out_specs=pl.BlockSpec((None, tm, D_MODEL), lambda b, i: (b, i, 0)),
        out_shape=jax.ShapeDtypeStruct((bsz, seq, D_MODEL), F32),
        scratch_shapes=[pltpu.VMEM((tm, D_MODEL), F32),
                        pltpu.VMEM((ext, D_MODEL), BF16),
                        pltpu.VMEM((2, ext, FF_CHUNK), F32),
                        pltpu.VMEM((2, ext, FF_CHUNK), F32),
                        pltpu.VMEM((tm, D_MODEL), F32)],
        compiler_params=pltpu.CompilerParams(
            dimension_semantics=("parallel", "arbitrary"), vmem_limit_bytes=VMEM_LIMIT_BYTES),
        name="post",
    )(h, h, h, o_attn, o_attn, o_attn, o_conv, o_conv, o_conv, p,
      lw["w_out"], lw["ffn_norm_g"], lw["w_up"], lw["ffn_conv_w"], lw["w_down"],
      lw["w_ple_gate"], lw["w_ple_proj"])


def _rope_tables(seq):
    rows = seq // GRID_W
    row = jnp.repeat(jnp.arange(rows), GRID_W).astype(F32)
    col = jnp.tile(jnp.arange(GRID_W), rows).astype(F32)
    half = HEAD_DIM // 2
    freqs = ROPE_THETA ** (-jnp.arange(0, half, 2, dtype=F32) / half)
    ang_r = row[:, None] * freqs[None, :]
    ang_c = col[:, None] * freqs[None, :]
    ang = jnp.concatenate([ang_r, ang_r, ang_c, ang_c], axis=-1)
    cos, sin = jnp.cos(ang), jnp.sin(ang)
    first = (jnp.arange(HEAD_DIM) % (HEAD_DIM // 2)) < HEAD_DIM // 4
    sin_up = jnp.where(first, -sin, 0.0)
    sin_dn = jnp.where(first, 0.0, sin)
    widen = lambda t: jnp.tile(t, (1, LANES // HEAD_DIM))
    return widen(cos), widen(sin_up), widen(sin_dn)


def _layer_weights(i, attn_norm_g, w_in, q_norm_g, k_norm_g, mix_conv_w, attn_out_g, conv_out_g,
                   w_out, ffn_norm_g, w_up, ffn_conv_w, w_down, w_ple_gate, w_ple_proj):
    head = jnp.arange(ATTN_W) // HEAD_DIM
    return {
        "attn_norm_g": attn_norm_g[i][None],
        "w_in": w_in[i].astype(BF16),
        "q_gain": jnp.tile(q_norm_g[i] * (math.log2(math.e) / math.sqrt(HEAD_DIM)), N_HEADS)[None],
        "k_gain": jnp.tile(k_norm_g[i], N_KV_HEADS)[None],
        "head_group": jnp.where(head[:, None] == head[None, :], 1.0 / HEAD_DIM, 0.0).astype(BF16),
        "mix_conv_w": mix_conv_w[i],
        "attn_out_g": attn_out_g[i][None],
        "conv_out_g": conv_out_g[i][None],
        "w_out": w_out[i].astype(BF16),
        "ffn_norm_g": ffn_norm_g[i][None],
        "w_up": w_up[i].astype(BF16),
        "ffn_conv_w": ffn_conv_w[i],
        "w_down": w_down[i].astype(BF16).reshape(N_FF_CHUNKS, FF_CHUNK, D_MODEL),
        "w_ple_gate": w_ple_gate[i].astype(BF16),
        "w_ple_proj": w_ple_proj[i].astype(BF16),
    }


def _trunk(x, p, layers):
    tables = _rope_tables(x.shape[1])
    h = x
    for i, lw in enumerate(layers):
        qt, k, vt, o_conv = _mixer_in(h, lw, tables)
        o_attn = _attention(qt, k, vt, lw["attn_out_g"])
        h = _post(h, o_attn, o_conv, p, i, lw)
    return h


def kernel(x_prompt, x_sample, p_prompt, p_sample, attn_norm_g, w_in, q_norm_g, k_norm_g, mix_conv_w,
           attn_out_g, conv_out_g, w_out, ffn_norm_g, w_up, ffn_conv_w, w_down, w_ple_gate, w_ple_proj):
    depth = w_in.shape[0]
    layers = [_layer_weights(i, attn_norm_g, w_in, q_norm_g, k_norm_g, mix_conv_w, attn_out_g,
                             conv_out_g, w_out, ffn_norm_g, w_up, ffn_conv_w, w_down, w_ple_gate,
                             w_ple_proj) for i in range(depth)]
    return (_trunk(x_prompt, p_prompt, layers), _trunk(x_sample, p_sample, layers))
```

```python
import functools
import math

import jax
import jax.numpy as jnp
from jax import lax
from jax.experimental import pallas as pl
from jax.experimental.pallas import tpu as pltpu

D_MODEL = 1024
GRID_W = 64
HEAD_DIM = 64
N_HEADS = 8
N_KV_HEADS = 2
Q_PER_KV = N_HEADS // N_KV_HEADS
ATTN_W = N_HEADS * HEAD_DIM
KV_W = N_KV_HEADS * HEAD_DIM
CONV_W = 512
D_IN = ATTN_W + 2 * KV_W + 3 * CONV_W
D_FF = 2816
PLE_DIM = 256
ROPE_THETA = 10000.0
EPS = 1e-6

LANES = 128
F32_ROWS = 8
BF16_ROWS = 16
VMEM_LIMIT_BYTES = 56 * 1024 * 1024

TM_IN = 1024
TQ = 256
N_KEY_CHUNKS = 4
SCORE_LOOKAHEAD = 2
V_ROWS = HEAD_DIM + BF16_ROWS
SCORE_BOUND = 50.0
ROUNDING_SLACK = 1.02
TM_POST = 512
FF_CHUNK = 256
N_FF_CHUNKS = D_FF // FF_CHUNK
assert N_FF_CHUNKS * FF_CHUNK == D_FF and N_FF_CHUNKS % 2 == 1

F32 = jnp.float32
BF16 = jnp.bfloat16


def _resident(shape):
    zeros = (0,) * len(shape)
    return pl.BlockSpec(shape, lambda *_: zeros, pipeline_mode=pl.Buffered(1))


def _rms(x, g):
    return x * lax.rsqrt(jnp.mean(x * x, axis=-1, keepdims=True) + EPS) * g


def _rope(x, cos, sin_up, sin_dn):
    cols = []
    for c in range(x.shape[1] // LANES):
        xc = x[:, c * LANES:(c + 1) * LANES]
        up = pltpu.roll(xc, LANES - HEAD_DIM // 4, 1)
        dn = pltpu.roll(xc, HEAD_DIM // 4, 1)
        cols.append(xc * cos + up * sin_up + dn * sin_dn)
    return jnp.concatenate(cols, axis=1) if len(cols) > 1 else cols[0]


def _edge_mask(rows, tm, is_first, is_last):
    r = lax.broadcasted_iota(jnp.int32, (rows, 1), 0)
    outside = jnp.logical_or(jnp.logical_and(r < F32_ROWS, is_first),
                             jnp.logical_and(r >= F32_ROWS + tm, is_last))
    return jnp.logical_not(outside)


def _dwconv3(ref, w, r0, nrows):
    x = ref[r0:r0 + nrows + 2 * F32_ROWS, :]
    lo = F32_ROWS
    return (pltpu.roll(x, 1, 0)[lo:lo + nrows] * w[0:1, :] + x[lo:lo + nrows] * w[1:2, :]
            + pltpu.roll(x, x.shape[0] - 1, 0)[lo:lo + nrows] * w[2:3, :])


def _mixer_in_kernel(h_ref, hp_ref, hn_ref, ng_ref, w_ref, qg_ref, kg_ref, cos_ref, su_ref, sd_ref,
                     grp_ref, cw_ref, cg_ref,
                     qt_ref, k_ref, vt_ref, oc_ref, cu_ref):
    i = pl.program_id(1)
    tm = h_ref.shape[0]
    lo, hi = F32_ROWS, F32_ROWS + tm
    x = jnp.concatenate([hp_ref[...], h_ref[...], hn_ref[...]], axis=0)
    a = _rms(x, ng_ref[...]).astype(BF16)

    def proj(c0, c1):
        return jnp.dot(a, w_ref[:, c0:c1], preferred_element_type=F32)

    cos, s_up, s_dn = cos_ref[...], su_ref[...], sd_ref[...]

    def head_norm(z, g):
        w = z.shape[1]
        ms = jnp.dot((z * z).astype(BF16), grp_ref[:w, :w], preferred_element_type=F32)
        return z * lax.rsqrt(ms + EPS) * g

    c0 = ATTN_W + 2 * KV_W
    cu = proj(c0 + CONV_W, c0 + 2 * CONV_W) * proj(c0 + 2 * CONV_W, c0 + 3 * CONV_W)
    keep = _edge_mask(cu.shape[0], tm, i == 0, i == pl.num_programs(1) - 1)
    cu_ref[...] = jnp.where(keep, cu, 0.0)

    zq = proj(0, ATTN_W)[lo:hi]
    conv = _dwconv3(cu_ref, cw_ref[...], 0, tm)
    qt_ref[...] = _rope(head_norm(zq, qg_ref[...]), cos, s_up, s_dn).T.astype(BF16)

    zkv = proj(ATTN_W, ATTN_W + 2 * KV_W)[lo:hi]
    k = _rope(head_norm(zkv[:, :KV_W], kg_ref[...]), cos, s_up, s_dn).astype(BF16)
    for g in range(N_KV_HEADS):
        k_ref[g] = k[:, g * HEAD_DIM:(g + 1) * HEAD_DIM]
    vt = zkv[:, KV_W:].T.astype(BF16)
    for g in range(N_KV_HEADS):
        vt_ref[g * V_ROWS:g * V_ROWS + HEAD_DIM, :] = vt[g * HEAD_DIM:(g + 1) * HEAD_DIM]
        vt_ref[g * V_ROWS + HEAD_DIM:(g + 1) * V_ROWS, :] = jnp.ones((BF16_ROWS, tm), BF16)

    gb = proj(c0, c0 + CONV_W)[lo:hi]
    oc_ref[...] = _rms(gb * conv, cg_ref[...]).astype(BF16)


def _halo_specs(tm, seq, width, rows):
    per = tm // rows
    last = seq // rows - 1
    return [
        pl.BlockSpec((None, tm, width), lambda b, i: (b, i, 0)),
        pl.BlockSpec((None, rows, width), lambda b, i: (b, jnp.maximum(i * per - 1, 0), 0)),
        pl.BlockSpec((None, rows, width), lambda b, i: (b, jnp.minimum((i + 1) * per, last), 0)),
    ]


def _mixer_in(h, lw, tables):
    bsz, seq, _ = h.shape
    tm = min(TM_IN, seq)
    cos, s_up, s_dn = tables
    tab_spec = pl.BlockSpec((tm, LANES), lambda b, i: (i, 0))
    tok = lambda w: pl.BlockSpec((None, tm, w), lambda b, i: (b, i, 0))
    tok_t = lambda w: pl.BlockSpec((None, w, tm), lambda b, i: (b, 0, i))
    return pl.pallas_call(
        _mixer_in_kernel,
        grid=(bsz, seq // tm),
        in_specs=_halo_specs(tm, seq, D_MODEL, F32_ROWS) + [
            _resident((1, D_MODEL)), _resident((D_MODEL, D_IN)),
            _resident((1, ATTN_W)), _resident((1, KV_W)),
            tab_spec, tab_spec, tab_spec,
            _resident((ATTN_W, ATTN_W)), _resident((3, CONV_W)), _resident((1, CONV_W)),
        ],
        out_specs=[tok_t(ATTN_W),
                   pl.BlockSpec((None, N_KV_HEADS, tm, HEAD_DIM), lambda b, i: (b, 0, i, 0)),
                   tok_t(N_KV_HEADS * V_ROWS), tok(CONV_W)],
        out_shape=[jax.ShapeDtypeStruct((bsz, ATTN_W, seq), BF16),
                   jax.ShapeDtypeStruct((bsz, N_KV_HEADS, seq, HEAD_DIM), BF16),
                   jax.ShapeDtypeStruct((bsz, N_KV_HEADS * V_ROWS, seq), BF16),
                   jax.ShapeDtypeStruct((bsz, seq, CONV_W), BF16)],
        scratch_shapes=[pltpu.VMEM((tm + 2 * F32_ROWS, CONV_W), F32)],
        compiler_params=pltpu.CompilerParams(
            dimension_semantics=("parallel", "arbitrary"), vmem_limit_bytes=VMEM_LIMIT_BYTES),
        name="mixer_in",
    )(h, h, h, lw["attn_norm_g"], lw["w_in"], lw["q_gain"], lw["k_gain"], cos, s_up, s_dn,
      lw["head_group"], lw["mix_conv_w"], lw["conv_out_g"])


def _attention_kernel(qt_ref, k_ref, vt_ref, og_ref, o_ref, m_ref, acc_ref, ot_ref, stage_ref, *, bounded):
    seq = k_ref.shape[1]
    n_chunks = N_KEY_CHUNKS
    tk = seq // n_chunks
    half = tk // 2

    if not bounded:
        m_ref[...] = jnp.full_like(m_ref, -jnp.inf)
    acc_ref[...] = jnp.zeros_like(acc_ref)

    def stage_scores(c, h):
        start = pl.multiple_of(c * tk, tk)
        for j in range(2):
            stage_ref[h, j] = jnp.dot(k_ref[h // Q_PER_KV, pl.ds(start + j * half, half), :],
                                      qt_ref[h * HEAD_DIM:(h + 1) * HEAD_DIM, :],
                                      preferred_element_type=F32)

    for h in range(SCORE_LOOKAHEAD):
        stage_scores(0, h)

    def chunk(c, wrap):
        start = pl.multiple_of(c * tk, tk)
        for h in range(N_HEADS):
            g = h // Q_PER_KV
            ahead = h + SCORE_LOOKAHEAD
            if ahead < N_HEADS:
                stage_scores(c, ahead)
            elif wrap:
                stage_scores(c + 1, ahead - N_HEADS)
            vt = vt_ref[g * V_ROWS:(g + 1) * V_ROWS, pl.ds(start, tk)]
            if bounded:
                pb = jnp.concatenate([jnp.exp2(stage_ref[h, j]).astype(BF16) for j in range(2)], axis=0)
                acc_ref[h] += jnp.dot(vt, pb, preferred_element_type=F32)
                continue
            m_old = m_ref[h]
            m_new = jnp.maximum(m_old, jnp.maximum(jnp.max(stage_ref[h, 0], axis=0, keepdims=True),
                                                   jnp.max(stage_ref[h, 1], axis=0, keepdims=True)))
            alpha = jnp.exp2(m_old - m_new)
            pb = jnp.concatenate([jnp.exp2(stage_ref[h, j] - m_new).astype(BF16) for j in range(2)],
                                 axis=0)
            acc_ref[h] = alpha * acc_ref[h] + jnp.dot(vt, pb, preferred_element_type=F32)
            m_ref[h] = m_new

    def wrapped(c, carry):
        chunk(c, True)
        return carry

    lax.fori_loop(0, n_chunks - 1, wrapped, 0)
    chunk(n_chunks - 1, False)

    for h in range(N_HEADS):
        ot_ref[h * HEAD_DIM:(h + 1) * HEAD_DIM, :] = (
            acc_ref[h, :HEAD_DIM, :] / acc_ref[h, HEAD_DIM:HEAD_DIM + 1, :])
    o_ref[...] = _rms(ot_ref[...].T, og_ref[...]).astype(BF16)


def _attention(qt, k, vt, out_gain, bounded):
    bsz, _, seq, _ = k.shape
    tq = min(TQ, seq)
    return pl.pallas_call(
        functools.partial(_attention_kernel, bounded=bounded),
        grid=(bsz, seq // tq),
        in_specs=[pl.BlockSpec((None, ATTN_W, tq), lambda b, i: (b, 0, i)),
                  pl.BlockSpec((None, N_KV_HEADS, seq, HEAD_DIM), lambda b, i: (b, 0, 0, 0)),
                  pl.BlockSpec((None, N_KV_HEADS * V_ROWS, seq), lambda b, i: (b, 0, 0)),
                  _resident((1, ATTN_W))],
        out_specs=pl.BlockSpec((None, tq, ATTN_W), lambda b, i: (b, i, 0)),
        out_shape=jax.ShapeDtypeStruct((bsz, seq, ATTN_W), BF16),
        scratch_shapes=[pltpu.VMEM((N_HEADS, 1, tq), F32),
                        pltpu.VMEM((N_HEADS, V_ROWS, tq), F32),
                        pltpu.VMEM((ATTN_W, tq), F32),
                        pltpu.VMEM((N_HEADS, 2, seq // N_KEY_CHUNKS // 2, tq), F32)],
        compiler_params=pltpu.CompilerParams(
            dimension_semantics=("parallel", "arbitrary"), vmem_limit_bytes=VMEM_LIMIT_BYTES),
        name="attention_bounded" if bounded else "attention_online",
    )(qt, k, vt, out_gain)


def _post_kernel(h_ref, hp_ref, hn_ref, oa_ref, oap_ref, oan_ref, oc_ref, ocp_ref, ocn_ref, p_ref,
                 wo_ref, fg_ref, wup_ref, fcw_ref, wdn_ref, wpg_ref, wpp_ref,
                 y_ref, hm_ref, a2_ref, ug_ref, uu_ref, acc_ref):
    i = pl.program_id(1)
    tm = h_ref.shape[0]
    lo, hi = F32_ROWS, F32_ROWS + tm

    def extended(main_ref, prev_ref, next_ref):
        prev = prev_ref[...].astype(F32)[BF16_ROWS - F32_ROWS:]
        nxt = next_ref[...].astype(F32)[:F32_ROWS]
        return jnp.concatenate([prev, main_ref[...].astype(F32), nxt], axis=0).astype(BF16)

    oa = extended(oa_ref, oap_ref, oan_ref)
    oc = extended(oc_ref, ocp_ref, ocn_ref)
    x = jnp.concatenate([hp_ref[...], h_ref[...], hn_ref[...]], axis=0)
    hm = (x + jnp.dot(oa, wo_ref[:ATTN_W, :], preferred_element_type=F32)
          + jnp.dot(oc, wo_ref[ATTN_W:, :], preferred_element_type=F32))
    hm_ref[...] = hm[lo:hi]
    keep = _edge_mask(hm.shape[0], tm, i == 0, i == pl.num_programs(1) - 1)
    a2_ref[...] = jnp.where(keep, _rms(hm, fg_ref[...]), 0.0).astype(BF16)
    acc_ref[...] = jnp.zeros_like(acc_ref)

    def ff_cols(which, c):
        return pl.ds(pl.multiple_of(which * D_FF + c * FF_CHUNK, FF_CHUNK), FF_CHUNK)

    def up_proj(c, slot):
        a2 = a2_ref[...]
        ug_ref[slot] = jnp.dot(a2, wup_ref[:, ff_cols(0, c)], preferred_element_type=F32)
        uu_ref[slot] = jnp.dot(a2, wup_ref[:, ff_cols(1, c)], preferred_element_type=F32)

    def act_down(c, slot):
        gate = _dwconv3(ug_ref.at[slot], fcw_ref[:, ff_cols(0, c)], 0, tm)
        lin = _dwconv3(uu_ref.at[slot], fcw_ref[:, ff_cols(1, c)], 0, tm)
        act = (gate * jax.nn.sigmoid(gate) * lin).astype(BF16)
        acc_ref[...] += jnp.dot(act, wdn_ref[c], preferred_element_type=F32)

    up_proj(0, 0)

    def ff_pair(i, carry):
        c = 2 * i
        up_proj(c + 1, 1)
        act_down(c, 0)
        up_proj(c + 2, 0)
        act_down(c + 1, 1)
        return carry

    lax.fori_loop(0, (N_FF_CHUNKS - 1) // 2, ff_pair, 0)
    act_down(N_FF_CHUNKS - 1, 0)

    h2 = hm_ref[...] + acc_ref[...]
    gate = jax.nn.sigmoid(jnp.dot(h2.astype(BF16), wpg_ref[...], preferred_element_type=F32))
    emb = jnp.dot(p_ref[...].astype(BF16), wpp_ref[...], preferred_element_type=F32)
    y_ref[...] = h2 + gate * emb


def _post(h, o_attn, o_conv, p, layer, lw):
    bsz, seq, _ = h.shape
    tm = min(TM_POST, seq)
    ext = tm + 2 * F32_ROWS
    return pl.pallas_call(
        _post_kernel,
        grid=(bsz, seq // tm),
        in_specs=(_halo_specs(tm, seq, D_MODEL, F32_ROWS)
                  + _halo_specs(tm, seq, ATTN_W, BF16_ROWS)
                  + _halo_specs(tm, seq, CONV_W, BF16_ROWS)
                  + [pl.BlockSpec((None, None, tm, PLE_DIM), lambda b, i: (layer, b, i, 0)),
                     _resident((ATTN_W + CONV_W, D_MODEL)), _resident((1, D_MODEL)),
                     _resident((D_MODEL, 2 * D_FF)),
                     _resident((3, 2 * D_FF)),
                     _resident((N_FF_CHUNKS, FF_CHUNK, D_MODEL)),
                     _resident((D_MODEL, D_MODEL)), _resident((PLE_DIM, D_MODEL))]),
        out_specs=pl.BlockSpec((None, tm, D_MODEL), lambda b, i: (b, i, 0)),
        out_shape=jax.ShapeDtypeStruct((bsz, seq, D_MODEL), F32),
        scratch_shapes=[pltpu.VMEM((tm, D_MODEL), F32),
                        pltpu.VMEM((ext, D_MODEL), BF16),
                        pltpu.VMEM((2, ext, FF_CHUNK), F32),
                        pltpu.VMEM((2, ext, FF_CHUNK), F32),
                        pltpu.VMEM((tm, D_MODEL), F32)],
        compiler_params=pltpu.CompilerParams(
            dimension_semantics=("parallel", "arbitrary"), vmem_limit_bytes=VMEM_LIMIT_BYTES),
        name="post",
    )(h, h, h, o_attn, o_attn, o_attn, o_conv, o_conv, o_conv, p,
      lw["w_out"], lw["ffn_norm_g"], lw["w_up"], lw["ffn_conv_w"], lw["w_down"],
      lw["w_ple_gate"], lw["w_ple_proj"])


def _rope_tables(seq):
    rows = seq // GRID_W
    row = jnp.repeat(jnp.arange(rows), GRID_W).astype(F32)
    col = jnp.tile(jnp.arange(GRID_W), rows).astype(F32)
    half = HEAD_DIM // 2
    freqs = ROPE_THETA ** (-jnp.arange(0, half, 2, dtype=F32) / half)
    ang_r = row[:, None] * freqs[None, :]
    ang_c = col[:, None] * freqs[None, :]
    ang = jnp.concatenate([ang_r, ang_r, ang_c, ang_c], axis=-1)
    cos, sin = jnp.cos(ang), jnp.sin(ang)
    first = (jnp.arange(HEAD_DIM) % (HEAD_DIM // 2)) < HEAD_DIM // 4
    sin_up = jnp.where(first, -sin, 0.0)
    sin_dn = jnp.where(first, 0.0, sin)
    widen = lambda t: jnp.tile(t, (1, LANES // HEAD_DIM))
    return widen(cos), widen(sin_up), widen(sin_dn)


def _layer_weights(i, attn_norm_g, w_in, q_norm_g, k_norm_g, mix_conv_w, attn_out_g, conv_out_g,
                   w_out, ffn_norm_g, w_up, ffn_conv_w, w_down, w_ple_gate, w_ple_proj):
    head = jnp.arange(ATTN_W) // HEAD_DIM
    q_scale = math.log2(math.e) / math.sqrt(HEAD_DIM)
    return {
        "attn_norm_g": attn_norm_g[i][None],
        "w_in": w_in[i].astype(BF16),
        "q_gain": jnp.tile(q_norm_g[i] * q_scale, N_HEADS)[None],
        "k_gain": jnp.tile(k_norm_g[i], N_KV_HEADS)[None],
        "scores_bounded": (HEAD_DIM * ROUNDING_SLACK * jnp.max(jnp.abs(q_norm_g[i] * q_scale))
                           * jnp.max(jnp.abs(k_norm_g[i])) <= SCORE_BOUND),
        "head_group": jnp.where(head[:, None] == head[None, :], 1.0 / HEAD_DIM, 0.0).astype(BF16),
        "mix_conv_w": mix_conv_w[i],
        "attn_out_g": attn_out_g[i][None],
        "conv_out_g": conv_out_g[i][None],
        "w_out": w_out[i].astype(BF16),
        "ffn_norm_g": ffn_norm_g[i][None],
        "w_up": w_up[i].astype(BF16),
        "ffn_conv_w": ffn_conv_w[i],
        "w_down": w_down[i].astype(BF16).reshape(N_FF_CHUNKS, FF_CHUNK, D_MODEL),
        "w_ple_gate": w_ple_gate[i].astype(BF16),
        "w_ple_proj": w_ple_proj[i].astype(BF16),
    }


def _trunk(x, p, layers):
    tables = _rope_tables(x.shape[1])
    h = x
    for i, lw in enumerate(layers):
        qt, k, vt, o_conv = _mixer_in(h, lw, tables)
        o_attn = lax.cond(lw["scores_bounded"],
                          functools.partial(_attention, bounded=True),
                          functools.partial(_attention, bounded=False),
                          qt, k, vt, lw["attn_out_g"])
        h = _post(h, o_attn, o_conv, p, i, lw)
    return h


def kernel(x_prompt, x_sample, p_prompt, p_sample, attn_norm_g, w_in, q_norm_g, k_norm_g, mix_conv_w,
           attn_out_g, conv_out_g, w_out, ffn_norm_g, w_up, ffn_conv_w, w_down, w_ple_gate, w_ple_proj):
    depth = w_in.shape[0]
    layers = [_layer_weights(i, attn_norm_g, w_in, q_norm_g, k_norm_g, mix_conv_w, attn_out_g,
                             conv_out_g, w_out, ffn_norm_g, w_up, ffn_conv_w, w_down, w_ple_gate,
                             w_ple_proj) for i in range(depth)]
    return (_trunk(x_prompt, p_prompt, layers), _trunk(x_sample, p_sample, layers))
```

```python
import functools
import math

import jax
import jax.numpy as jnp
from jax import lax
from jax.experimental import pallas as pl
from jax.experimental.pallas import tpu as pltpu

D_MODEL = 1024
GRID_W = 64
HEAD_DIM = 64
N_HEADS = 8
N_KV_HEADS = 2
Q_PER_KV = N_HEADS // N_KV_HEADS
ATTN_W = N_HEADS * HEAD_DIM
KV_W = N_KV_HEADS * HEAD_DIM
CONV_W = 512
D_IN = ATTN_W + 2 * KV_W + 3 * CONV_W
D_FF = 2816
PLE_DIM = 256
ROPE_THETA = 10000.0
EPS = 1e-6

LANES = 128
F32_ROWS = 8
BF16_ROWS = 16
VMEM_LIMIT_BYTES = 56 * 1024 * 1024

TM_IN = 1024
TQ = 512
N_KEY_CHUNKS = 2
SCORE_LOOKAHEAD = 2
V_ROWS = HEAD_DIM + BF16_ROWS
SCORE_BOUND = 50.0
ROUNDING_SLACK = 1.02
TM_POST = 512
FF_CHUNK = 256
N_FF_CHUNKS = D_FF // FF_CHUNK
assert N_FF_CHUNKS * FF_CHUNK == D_FF and N_FF_CHUNKS % 2 == 1

F32 = jnp.float32
BF16 = jnp.bfloat16


def _resident(shape):
    zeros = (0,) * len(shape)
    return pl.BlockSpec(shape, lambda *_: zeros, pipeline_mode=pl.Buffered(1))


def _rms(x, g):
    return x * lax.rsqrt(jnp.mean(x * x, axis=-1, keepdims=True) + EPS) * g


def _rope(x, cos, sin_up, sin_dn):
    cols = []
    for c in range(x.shape[1] // LANES):
        xc = x[:, c * LANES:(c + 1) * LANES]
        up = pltpu.roll(xc, LANES - HEAD_DIM // 4, 1)
        dn = pltpu.roll(xc, HEAD_DIM // 4, 1)
        cols.append(xc * cos + up * sin_up + dn * sin_dn)
    return jnp.concatenate(cols, axis=1) if len(cols) > 1 else cols[0]


def _edge_mask(rows, tm, is_first, is_last):
    r = lax.broadcasted_iota(jnp.int32, (rows, 1), 0)
    outside = jnp.logical_or(jnp.logical_and(r < F32_ROWS, is_first),
                             jnp.logical_and(r >= F32_ROWS + tm, is_last))
    return jnp.logical_not(outside)


def _dwconv3(ref, w, r0, nrows):
    x = ref[r0:r0 + nrows + 2 * F32_ROWS, :]
    lo = F32_ROWS
    return (pltpu.roll(x, 1, 0)[lo:lo + nrows] * w[0:1, :] + x[lo:lo + nrows] * w[1:2, :]
            + pltpu.roll(x, x.shape[0] - 1, 0)[lo:lo + nrows] * w[2:3, :])


def _mixer_in_kernel(h_ref, hp_ref, hn_ref, ng_ref, w_ref, qg_ref, kg_ref, cos_ref, su_ref, sd_ref,
                     grp_ref, cw_ref, cg_ref,
                     qt_ref, k_ref, vt_ref, oc_ref, cu_ref):
    i = pl.program_id(1)
    tm = h_ref.shape[0]
    lo, hi = F32_ROWS, F32_ROWS + tm
    x = jnp.concatenate([hp_ref[...], h_ref[...], hn_ref[...]], axis=0)
    a = _rms(x, ng_ref[...]).astype(BF16)

    def proj(c0, c1):
        return jnp.dot(a, w_ref[:, c0:c1], preferred_element_type=F32)

    cos, s_up, s_dn = cos_ref[...], su_ref[...], sd_ref[...]

    def head_norm(z, g):
        w = z.shape[1]
        ms = jnp.dot((z * z).astype(BF16), grp_ref[:w, :w], preferred_element_type=F32)
        return z * lax.rsqrt(ms + EPS) * g

    c0 = ATTN_W + 2 * KV_W
    cu = proj(c0 + CONV_W, c0 + 2 * CONV_W) * proj(c0 + 2 * CONV_W, c0 + 3 * CONV_W)
    keep = _edge_mask(cu.shape[0], tm, i == 0, i == pl.num_programs(1) - 1)
    cu_ref[...] = jnp.where(keep, cu, 0.0)

    zq = proj(0, ATTN_W)[lo:hi]
    conv = _dwconv3(cu_ref, cw_ref[...], 0, tm)
    qt_ref[...] = _rope(head_norm(zq, qg_ref[...]), cos, s_up, s_dn).T.astype(BF16)

    zkv = proj(ATTN_W, ATTN_W + 2 * KV_W)[lo:hi]
    k = _rope(head_norm(zkv[:, :KV_W], kg_ref[...]), cos, s_up, s_dn).astype(BF16)
    for g in range(N_KV_HEADS):
        k_ref[g] = k[:, g * HEAD_DIM:(g + 1) * HEAD_DIM]
    vt = zkv[:, KV_W:].T.astype(BF16)
    for g in range(N_KV_HEADS):
        vt_ref[g * V_ROWS:g * V_ROWS + HEAD_DIM, :] = vt[g * HEAD_DIM:(g + 1) * HEAD_DIM]
        vt_ref[g * V_ROWS + HEAD_DIM:(g + 1) * V_ROWS, :] = jnp.ones((BF16_ROWS, tm), BF16)

    gb = proj(c0, c0 + CONV_W)[lo:hi]
    oc_ref[...] = _rms(gb * conv, cg_ref[...]).astype(BF16)


def _halo_specs(tm, seq, width, rows):
    per = tm // rows
    last = seq // rows - 1
    return [
        pl.BlockSpec((None, tm, width), lambda b, i: (b, i, 0)),
        pl.BlockSpec((None, rows, width), lambda b, i: (b, jnp.maximum(i * per - 1, 0), 0)),
        pl.BlockSpec((None, rows, width), lambda b, i: (b, jnp.minimum((i + 1) * per, last), 0)),
    ]


def _mixer_in(h, lw, tables):
    bsz, seq, _ = h.shape
    tm = min(TM_IN, seq)
    cos, s_up, s_dn = tables
    tab_spec = pl.BlockSpec((tm, LANES), lambda b, i: (i, 0))
    tok = lambda w: pl.BlockSpec((None, tm, w), lambda b, i: (b, i, 0))
    tok_t = lambda w: pl.BlockSpec((None, w, tm), lambda b, i: (b, 0, i))
    return pl.pallas_call(
        _mixer_in_kernel,
        grid=(bsz, seq // tm),
        in_specs=_halo_specs(tm, seq, D_MODEL, F32_ROWS) + [
            _resident((1, D_MODEL)), _resident((D_MODEL, D_IN)),
            _resident((1, ATTN_W)), _resident((1, KV_W)),
            tab_spec, tab_spec, tab_spec,
            _resident((ATTN_W, ATTN_W)), _resident((3, CONV_W)), _resident((1, CONV_W)),
        ],
        out_specs=[tok_t(ATTN_W),
                   pl.BlockSpec((None, N_KV_HEADS, tm, HEAD_DIM), lambda b, i: (b, 0, i, 0)),
                   tok_t(N_KV_HEADS * V_ROWS), tok(CONV_W)],
        out_shape=[jax.ShapeDtypeStruct((bsz, ATTN_W, seq), BF16),
                   jax.ShapeDtypeStruct((bsz, N_KV_HEADS, seq, HEAD_DIM), BF16),
                   jax.ShapeDtypeStruct((bsz, N_KV_HEADS * V_ROWS, seq), BF16),
                   jax.ShapeDtypeStruct((bsz, seq, CONV_W), BF16)],
        scratch_shapes=[pltpu.VMEM((tm + 2 * F32_ROWS, CONV_W), F32)],
        compiler_params=pltpu.CompilerParams(
            dimension_semantics=("parallel", "arbitrary"), vmem_limit_bytes=VMEM_LIMIT_BYTES),
        name="mixer_in",
    )(h, h, h, lw["attn_norm_g"], lw["w_in"], lw["q_gain"], lw["k_gain"], cos, s_up, s_dn,
      lw["head_group"], lw["mix_conv_w"], lw["conv_out_g"])


def _attention_kernel(qt_ref, k_ref, vt_ref, og_ref, o_ref, m_ref, acc_ref, ot_ref, stage_ref, *, bounded):
    seq = k_ref.shape[1]
    n_chunks = N_KEY_CHUNKS
    tk = seq // n_chunks
    half = tk // 2

    if not bounded:
        m_ref[...] = jnp.full_like(m_ref, -jnp.inf)
    acc_ref[...] = jnp.zeros_like(acc_ref)

    def stage_scores(c, h):
        start = pl.multiple_of(c * tk, tk)
        for j in range(2):
            stage_ref[h, j] = jnp.dot(k_ref[h // Q_PER_KV, pl.ds(start + j * half, half), :],
                                      qt_ref[h * HEAD_DIM:(h + 1) * HEAD_DIM, :],
                                      preferred_element_type=F32)

    for h in range(SCORE_LOOKAHEAD):
        stage_scores(0, h)

    def chunk(c, wrap):
        start = pl.multiple_of(c * tk, tk)
        for h in range(N_HEADS):
            g = h // Q_PER_KV
            ahead = h + SCORE_LOOKAHEAD
            if ahead < N_HEADS:
                stage_scores(c, ahead)
            elif wrap:
                stage_scores(c + 1, ahead - N_HEADS)
            vt = vt_ref[g * V_ROWS:(g + 1) * V_ROWS, pl.ds(start, tk)]
            if bounded:
                pb = jnp.concatenate([jnp.exp2(stage_ref[h, j]).astype(BF16) for j in range(2)], axis=0)
                acc_ref[h] += jnp.dot(vt, pb, preferred_element_type=F32)
                continue
            m_old = m_ref[h]
            m_new = jnp.maximum(m_old, jnp.maximum(jnp.max(stage_ref[h, 0], axis=0, keepdims=True),
                                                   jnp.max(stage_ref[h, 1], axis=0, keepdims=True)))
            alpha = jnp.exp2(m_old - m_new)
            pb = jnp.concatenate([jnp.exp2(stage_ref[h, j] - m_new).astype(BF16) for j in range(2)],
                                 axis=0)
            acc_ref[h] = alpha * acc_ref[h] + jnp.dot(vt, pb, preferred_element_type=F32)
            m_ref[h] = m_new

    def wrapped(c, carry):
        chunk(c, True)
        return carry

    lax.fori_loop(0, n_chunks - 1, wrapped, 0)
    chunk(n_chunks - 1, False)

    for h in range(N_HEADS):
        ot_ref[h * HEAD_DIM:(h + 1) * HEAD_DIM, :] = (
            acc_ref[h, :HEAD_DIM, :] / acc_ref[h, HEAD_DIM:HEAD_DIM + 1, :])
    o_ref[...] = _rms(ot_ref[...].T, og_ref[...]).astype(BF16)


def _attention(qt, k, vt, out_gain, bounded):
    bsz, _, seq, _ = k.shape
    tq = min(TQ, seq)
    return pl.pallas_call(
        functools.partial(_attention_kernel, bounded=bounded),
        grid=(bsz, seq // tq),
        in_specs=[pl.BlockSpec((None, ATTN_W, tq), lambda b, i: (b, 0, i)),
                  pl.BlockSpec((None, N_KV_HEADS, seq, HEAD_DIM), lambda b, i: (b, 0, 0, 0)),
                  pl.BlockSpec((None, N_KV_HEADS * V_ROWS, seq), lambda b, i: (b, 0, 0)),
                  _resident((1, ATTN_W))],
        out_specs=pl.BlockSpec((None, tq, ATTN_W), lambda b, i: (b, i, 0)),
        out_shape=jax.ShapeDtypeStruct((bsz, seq, ATTN_W), BF16),
        scratch_shapes=[pltpu.VMEM((N_HEADS, 1, tq), F32),
                        pltpu.VMEM((N_HEADS, V_ROWS, tq), F32),
                        pltpu.VMEM((ATTN_W, tq), F32),
                        pltpu.VMEM((N_HEADS, 2, seq // N_KEY_CHUNKS // 2, tq), F32)],
        compiler_params=pltpu.CompilerParams(
            dimension_semantics=("parallel", "arbitrary"), vmem_limit_bytes=VMEM_LIMIT_BYTES),
        name="attention_bounded" if bounded else "attention_online",
    )(qt, k, vt, out_gain)


def _post_kernel(h_ref, hp_ref, hn_ref, oa_ref, oap_ref, oan_ref, oc_ref, ocp_ref, ocn_ref, p_ref,
                 wo_ref, fg_ref, wup_ref, fcw_ref, wdn_ref, wpg_ref, wpp_ref,
                 y_ref, hm_ref, a2_ref, ug_ref, uu_ref, acc_ref):
    i = pl.program_id(1)
    tm = h_ref.shape[0]
    lo, hi = F32_ROWS, F32_ROWS + tm

    def extended(main_ref, prev_ref, next_ref):
        prev = prev_ref[...].astype(F32)[BF16_ROWS - F32_ROWS:]
        nxt = next_ref[...].astype(F32)[:F32_ROWS]
        return jnp.concatenate([prev, main_ref[...].astype(F32), nxt], axis=0).astype(BF16)

    oa = extended(oa_ref, oap_ref, oan_ref)
    oc = extended(oc_ref, ocp_ref, ocn_ref)
    x = jnp.concatenate([hp_ref[...], h_ref[...], hn_ref[...]], axis=0)
    hm = (x + jnp.dot(oa, wo_ref[:ATTN_W, :], preferred_element_type=F32)
          + jnp.dot(oc, wo_ref[ATTN_W:, :], preferred_element_type=F32))
    hm_ref[...] = hm[lo:hi]
    keep = _edge_mask(hm.shape[0], tm, i == 0, i == pl.num_programs(1) - 1)
    a2_ref[...] = jnp.where(keep, _rms(hm, fg_ref[...]), 0.0).astype(BF16)
    acc_ref[...] = jnp.zeros_like(acc_ref)

    def ff_cols(which, c):
        return pl.ds(pl.multiple_of(which * D_FF + c * FF_CHUNK, FF_CHUNK), FF_CHUNK)

    def up_proj(c, slot):
        a2 = a2_ref[...]
        ug_ref[slot] = jnp.dot(a2, wup_ref[:, ff_cols(0, c)], preferred_element_type=F32)
        uu_ref[slot] = jnp.dot(a2, wup_ref[:, ff_cols(1, c)], preferred_element_type=F32)

    def act_down(c, slot):
        gate = _dwconv3(ug_ref.at[slot], fcw_ref[:, ff_cols(0, c)], 0, tm)
        lin = _dwconv3(uu_ref.at[slot], fcw_ref[:, ff_cols(1, c)], 0, tm)
        act = (gate * jax.nn.sigmoid(gate) * lin).astype(BF16)
        acc_ref[...] += jnp.dot(act, wdn_ref[c], preferred_element_type=F32)

    up_proj(0, 0)

    def ff_pair(i, carry):
        c = 2 * i
        up_proj(c + 1, 1)
        act_down(c, 0)
        up_proj(c + 2, 0)
        act_down(c + 1, 1)
        return carry

    lax.fori_loop(0, (N_FF_CHUNKS - 1) // 2, ff_pair, 0)
    act_down(N_FF_CHUNKS - 1, 0)

    h2 = hm_ref[...] + acc_ref[...]
    gate = jax.nn.sigmoid(jnp.dot(h2.astype(BF16), wpg_ref[...], preferred_element_type=F32))
    emb = jnp.dot(p_ref[...].astype(BF16), wpp_ref[...], preferred_element_type=F32)
    y_ref[...] = h2 + gate * emb


def _post(h, o_attn, o_conv, p, layer, lw):
    bsz, seq, _ = h.shape
    tm = min(TM_POST, seq)
    ext = tm + 2 * F32_ROWS
    return pl.pallas_call(
        _post_kernel,
        grid=(bsz, seq // tm),
        in_specs=(_halo_specs(tm, seq, D_MODEL, F32_ROWS)
                  + _halo_specs(tm, seq, ATTN_W, BF16_ROWS)
                  + _halo_specs(tm, seq, CONV_W, BF16_ROWS)
                  + [pl.BlockSpec((None, None, tm, PLE_DIM), lambda b, i: (layer, b, i, 0)),
                     _resident((ATTN_W + CONV_W, D_MODEL)), _resident((1, D_MODEL)),
                     _resident((D_MODEL, 2 * D_FF)),
                     _resident((3, 2 * D_FF)),
                     _resident((N_FF_CHUNKS, FF_CHUNK, D_MODEL)),
                     _resident((D_MODEL, D_MODEL)), _resident((PLE_DIM, D_MODEL))]),
        out_specs=pl.BlockSpec((None, tm, D_MODEL), lambda b, i: (b, i, 0)),
        out_shape=jax.ShapeDtypeStruct((bsz, seq, D_MODEL), F32),
        scratch_shapes=[pltpu.VMEM((tm, D_MODEL), F32),
                        pltpu.VMEM((ext, D_MODEL), BF16),
                        pltpu.VMEM((2, ext, FF_CHUNK), F32),
                        pltpu.VMEM((2, ext, FF_CHUNK), F32),
                        pltpu.VMEM((tm, D_MODEL), F32)],
        compiler_params=pltpu.CompilerParams(
            dimension_semantics=("parallel", "arbitrary"), vmem_limit_bytes=VMEM_LIMIT_BYTES),
        name="post",
    )(h, h, h, o_attn, o_attn, o_attn, o_conv, o_conv, o_conv, p,
      lw["w_out"], lw["ffn_norm_g"], lw["w_up"], lw["ffn_conv_w"], lw["w_down"],
      lw["w_ple_gate"], lw["w_ple_proj"])


def _rope_tables(seq):
    rows = seq // GRID_W
    row = jnp.repeat(jnp.arange(rows), GRID_W).astype(F32)
    col = jnp.tile(jnp.arange(GRID_W), rows).astype(F32)
    half = HEAD_DIM // 2
    freqs = ROPE_THETA ** (-jnp.arange(0, half, 2, dtype=F32) / half)
    ang_r = row[:, None] * freqs[None, :]
    ang_c = col[:, None] * freqs[None, :]
    ang = jnp.concatenate([ang_r, ang_r, ang_c, ang_c], axis=-1)
    cos, sin = jnp.cos(ang), jnp.sin(ang)
    first = (jnp.arange(HEAD_DIM) % (HEAD_DIM // 2)) < HEAD_DIM // 4
    sin_up = jnp.where(first, -sin, 0.0)
    sin_dn = jnp.where(first, 0.0, sin)
    widen = lambda t: jnp.tile(t, (1, LANES // HEAD_DIM))
    return widen(cos), widen(sin_up), widen(sin_dn)


def _layer_weights(i, attn_norm_g, w_in, q_norm_g, k_norm_g, mix_conv_w, attn_out_g, conv_out_g,
                   w_out, ffn_norm_g, w_up, ffn_conv_w, w_down, w_ple_gate, w_ple_proj):
    head = jnp.arange(ATTN_W) // HEAD_DIM
    q_scale = math.log2(math.e) / math.sqrt(HEAD_DIM)
    return {
        "attn_norm_g": attn_norm_g[i][None],
        "w_in": w_in[i].astype(BF16),
        "q_gain": jnp.tile(q_norm_g[i] * q_scale, N_HEADS)[None],
        "k_gain": jnp.tile(k_norm_g[i], N_KV_HEADS)[None],
        "scores_bounded": (HEAD_DIM * ROUNDING_SLACK * jnp.max(jnp.abs(q_norm_g[i] * q_scale))
                           * jnp.max(jnp.abs(k_norm_g[i])) <= SCORE_BOUND),
        "head_group": jnp.where(head[:, None] == head[None, :], 1.0 / HEAD_DIM, 0.0).astype(BF16),
        "mix_conv_w": mix_conv_w[i],
        "attn_out_g": attn_out_g[i][None],
        "conv_out_g": conv_out_g[i][None],
        "w_out": w_out[i].astype(BF16),
        "ffn_norm_g": ffn_norm_g[i][None],
        "w_up": w_up[i].astype(BF16),
        "ffn_conv_w": ffn_conv_w[i],
        "w_down": w_down[i].astype(BF16).reshape(N_FF_CHUNKS, FF_CHUNK, D_MODEL),
        "w_ple_gate": w_ple_gate[i].astype(BF16),
        "w_ple_proj": w_ple_proj[i].astype(BF16),
    }


def _trunk(x, p, layers):
    tables = _rope_tables(x.shape[1])
    h = x
    for i, lw in enumerate(layers):
        qt, k, vt, o_conv = _mixer_in(h, lw, tables)
        o_attn = lax.cond(lw["scores_bounded"],
                          functools.partial(_attention, bounded=True),
                          functools.partial(_attention, bounded=False),
                          qt, k, vt, lw["attn_out_g"])
        h = _post(h, o_attn, o_conv, p, i, lw)
    return h


def kernel(x_prompt, x_sample, p_prompt, p_sample, attn_norm_g, w_in, q_norm_g, k_norm_g, mix_conv_w,
           attn_out_g, conv_out_g, w_out, ffn_norm_g, w_up, ffn_conv_w, w_down, w_ple_gate, w_ple_proj):
    depth = w_in.shape[0]
    layers = [_layer_weights(i, attn_norm_g, w_in, q_norm_g, k_norm_g, mix_conv_w, attn_out_g,
                             conv_out_g, w_out, ffn_norm_g, w_up, ffn_conv_w, w_down, w_ple_gate,
                             w_ple_proj) for i in range(depth)]
    return (_trunk(x_prompt, p_prompt, layers), _trunk(x_sample, p_sample, layers))
```

```python
import functools
import math

import jax
import jax.numpy as jnp
from jax import lax
from jax.experimental import pallas as pl
from jax.experimental.pallas import tpu as pltpu

D_MODEL = 1024
GRID_W = 64
HEAD_DIM = 64
N_HEADS = 8
N_KV_HEADS = 2
Q_PER_KV = N_HEADS // N_KV_HEADS
ATTN_W = N_HEADS * HEAD_DIM
KV_W = N_KV_HEADS * HEAD_DIM
CONV_W = 512
D_IN = ATTN_W + 2 * KV_W + 3 * CONV_W
D_FF = 2816
PLE_DIM = 256
ROPE_THETA = 10000.0
EPS = 1e-6

LANES = 128
F32_ROWS = 8
BF16_ROWS = 16
VMEM_LIMIT_BYTES = 56 * 1024 * 1024

TM_IN = 1024
TQ = 512
N_KEY_CHUNKS = 2
SCORE_LOOKAHEAD = 2
V_ROWS = HEAD_DIM + BF16_ROWS
SCORE_BOUND = 50.0
ROUNDING_SLACK = 1.02
TM_POST = 512
FF_CHUNK = 256
N_FF_CHUNKS = D_FF // FF_CHUNK
assert N_FF_CHUNKS * FF_CHUNK == D_FF

F32 = jnp.float32
BF16 = jnp.bfloat16


def _resident(shape):
    zeros = (0,) * len(shape)
    return pl.BlockSpec(shape, lambda *_: zeros, pipeline_mode=pl.Buffered(1))


def _rms(x, g):
    return x * lax.rsqrt(jnp.mean(x * x, axis=-1, keepdims=True) + EPS) * g


def _rope(x, cos, sin_up, sin_dn):
    cols = []
    for c in range(x.shape[1] // LANES):
        xc = x[:, c * LANES:(c + 1) * LANES]
        up = pltpu.roll(xc, LANES - HEAD_DIM // 4, 1)
        dn = pltpu.roll(xc, HEAD_DIM // 4, 1)
        cols.append(xc * cos + up * sin_up + dn * sin_dn)
    return jnp.concatenate(cols, axis=1) if len(cols) > 1 else cols[0]


def _edge_mask(rows, tm, is_first, is_last):
    r = lax.broadcasted_iota(jnp.int32, (rows, 1), 0)
    outside = jnp.logical_or(jnp.logical_and(r < F32_ROWS, is_first),
                             jnp.logical_and(r >= F32_ROWS + tm, is_last))
    return jnp.logical_not(outside)


def _dwconv3(ref, w, r0, nrows):
    x = ref[r0:r0 + nrows + 2 * F32_ROWS, :]
    lo = F32_ROWS
    return (pltpu.roll(x, 1, 0)[lo:lo + nrows] * w[0:1, :] + x[lo:lo + nrows] * w[1:2, :]
            + pltpu.roll(x, x.shape[0] - 1, 0)[lo:lo + nrows] * w[2:3, :])


def _mixer_in_kernel(h_ref, hp_ref, hn_ref, ng_ref, w_ref, qg_ref, kg_ref, cos_ref, su_ref, sd_ref,
                     grp_ref, cw_ref, cg_ref,
                     qt_ref, k_ref, vt_ref, oc_ref, cu_ref):
    i = pl.program_id(1)
    tm = h_ref.shape[0]
    lo, hi = F32_ROWS, F32_ROWS + tm
    x = jnp.concatenate([hp_ref[...], h_ref[...], hn_ref[...]], axis=0)
    a = _rms(x, ng_ref[...]).astype(BF16)

    def proj(c0, c1):
        return jnp.dot(a, w_ref[:, c0:c1], preferred_element_type=F32)

    cos, s_up, s_dn = cos_ref[...], su_ref[...], sd_ref[...]

    def head_norm(z, g):
        w = z.shape[1]
        ms = jnp.dot((z * z).astype(BF16), grp_ref[:w, :w], preferred_element_type=F32)
        return z * lax.rsqrt(ms + EPS) * g

    c0 = ATTN_W + 2 * KV_W
    cu = proj(c0 + CONV_W, c0 + 2 * CONV_W) * proj(c0 + 2 * CONV_W, c0 + 3 * CONV_W)
    keep = _edge_mask(cu.shape[0], tm, i == 0, i == pl.num_programs(1) - 1)
    cu_ref[...] = jnp.where(keep, cu, 0.0)

    zq = proj(0, ATTN_W)[lo:hi]
    conv = _dwconv3(cu_ref, cw_ref[...], 0, tm)
    qt_ref[...] = _rope(head_norm(zq, qg_ref[...]), cos, s_up, s_dn).T.astype(BF16)

    zkv = proj(ATTN_W, ATTN_W + 2 * KV_W)[lo:hi]
    k = _rope(head_norm(zkv[:, :KV_W], kg_ref[...]), cos, s_up, s_dn).astype(BF16)
    for g in range(N_KV_HEADS):
        k_ref[g] = k[:, g * HEAD_DIM:(g + 1) * HEAD_DIM]
    vt = zkv[:, KV_W:].T.astype(BF16)
    for g in range(N_KV_HEADS):
        vt_ref[g * V_ROWS:g * V_ROWS + HEAD_DIM, :] = vt[g * HEAD_DIM:(g + 1) * HEAD_DIM]
        vt_ref[g * V_ROWS + HEAD_DIM:(g + 1) * V_ROWS, :] = jnp.ones((BF16_ROWS, tm), BF16)

    gb = proj(c0, c0 + CONV_W)[lo:hi]
    oc_ref[...] = _rms(gb * conv, cg_ref[...]).astype(BF16)


def _halo_specs(tm, seq, width, rows):
    per = tm // rows
    last = seq // rows - 1
    return [
        pl.BlockSpec((None, tm, width), lambda b, i: (b, i, 0)),
        pl.BlockSpec((None, rows, width), lambda b, i: (b, jnp.maximum(i * per - 1, 0), 0)),
        pl.BlockSpec((None, rows, width), lambda b, i: (b, jnp.minimum((i + 1) * per, last), 0)),
    ]


def _mixer_in(h, lw, tables):
    bsz, seq, _ = h.shape
    tm = min(TM_IN, seq)
    cos, s_up, s_dn = tables
    tab_spec = pl.BlockSpec((tm, LANES), lambda b, i: (i, 0))
    tok = lambda w: pl.BlockSpec((None, tm, w), lambda b, i: (b, i, 0))
    tok_t = lambda w: pl.BlockSpec((None, w, tm), lambda b, i: (b, 0, i))
    return pl.pallas_call(
        _mixer_in_kernel,
        grid=(bsz, seq // tm),
        in_specs=_halo_specs(tm, seq, D_MODEL, F32_ROWS) + [
            _resident((1, D_MODEL)), _resident((D_MODEL, D_IN)),
            _resident((1, ATTN_W)), _resident((1, KV_W)),
            tab_spec, tab_spec, tab_spec,
            _resident((ATTN_W, ATTN_W)), _resident((3, CONV_W)), _resident((1, CONV_W)),
        ],
        out_specs=[tok_t(ATTN_W),
                   pl.BlockSpec((None, N_KV_HEADS, tm, HEAD_DIM), lambda b, i: (b, 0, i, 0)),
                   tok_t(N_KV_HEADS * V_ROWS), tok(CONV_W)],
        out_shape=[jax.ShapeDtypeStruct((bsz, ATTN_W, seq), BF16),
                   jax.ShapeDtypeStruct((bsz, N_KV_HEADS, seq, HEAD_DIM), BF16),
                   jax.ShapeDtypeStruct((bsz, N_KV_HEADS * V_ROWS, seq), BF16),
                   jax.ShapeDtypeStruct((bsz, seq, CONV_W), BF16)],
        scratch_shapes=[pltpu.VMEM((tm + 2 * F32_ROWS, CONV_W), F32)],
        compiler_params=pltpu.CompilerParams(
            dimension_semantics=("parallel", "arbitrary"), vmem_limit_bytes=VMEM_LIMIT_BYTES),
        name="mixer_in",
    )(h, h, h, lw["attn_norm_g"], lw["w_in"], lw["q_gain"], lw["k_gain"], cos, s_up, s_dn,
      lw["head_group"], lw["mix_conv_w"], lw["conv_out_g"])


def _attention_kernel(qt_ref, k_ref, vt_ref, og_ref, o_ref, m_ref, acc_ref, ot_ref, stage_ref, *, bounded):
    seq = k_ref.shape[1]
    n_chunks = N_KEY_CHUNKS
    tk = seq // n_chunks
    half = tk // 2

    if not bounded:
        m_ref[...] = jnp.full_like(m_ref, -jnp.inf)
    acc_ref[...] = jnp.zeros_like(acc_ref)

    def stage_scores(c, h):
        start = pl.multiple_of(c * tk, tk)
        for j in range(2):
            stage_ref[h, j] = jnp.dot(k_ref[h // Q_PER_KV, pl.ds(start + j * half, half), :],
                                      qt_ref[h * HEAD_DIM:(h + 1) * HEAD_DIM, :],
                                      preferred_element_type=F32)

    for h in range(SCORE_LOOKAHEAD):
        stage_scores(0, h)

    def chunk(c, wrap):
        start = pl.multiple_of(c * tk, tk)
        for h in range(N_HEADS):
            g = h // Q_PER_KV
            ahead = h + SCORE_LOOKAHEAD
            if ahead < N_HEADS:
                stage_scores(c, ahead)
            elif wrap:
                stage_scores(c + 1, ahead - N_HEADS)
            vt = vt_ref[g * V_ROWS:(g + 1) * V_ROWS, pl.ds(start, tk)]
            if bounded:
                pb = jnp.concatenate([jnp.exp2(stage_ref[h, j]).astype(BF16) for j in range(2)], axis=0)
                acc_ref[h] += jnp.dot(vt, pb, preferred_element_type=F32)
                continue
            m_old = m_ref[h]
            m_new = jnp.maximum(m_old, jnp.maximum(jnp.max(stage_ref[h, 0], axis=0, keepdims=True),
                                                   jnp.max(stage_ref[h, 1], axis=0, keepdims=True)))
            alpha = jnp.exp2(m_old - m_new)
            pb = jnp.concatenate([jnp.exp2(stage_ref[h, j] - m_new).astype(BF16) for j in range(2)],
                                 axis=0)
            acc_ref[h] = alpha * acc_ref[h] + jnp.dot(vt, pb, preferred_element_type=F32)
            m_ref[h] = m_new

    def wrapped(c, carry):
        chunk(c, True)
        return carry

    lax.fori_loop(0, n_chunks - 1, wrapped, 0)
    chunk(n_chunks - 1, False)

    for h in range(N_HEADS):
        ot_ref[h * HEAD_DIM:(h + 1) * HEAD_DIM, :] = (
            acc_ref[h, :HEAD_DIM, :] / acc_ref[h, HEAD_DIM:HEAD_DIM + 1, :])
    o_ref[...] = _rms(ot_ref[...].T, og_ref[...]).astype(BF16)


def _attention(qt, k, vt, out_gain, bounded):
    bsz, _, seq, _ = k.shape
    tq = min(TQ, seq)
    return pl.pallas_call(
        functools.partial(_attention_kernel, bounded=bounded),
        grid=(bsz, seq // tq),
        in_specs=[pl.BlockSpec((None, ATTN_W, tq), lambda b, i: (b, 0, i)),
                  pl.BlockSpec((None, N_KV_HEADS, seq, HEAD_DIM), lambda b, i: (b, 0, 0, 0)),
                  pl.BlockSpec((None, N_KV_HEADS * V_ROWS, seq), lambda b, i: (b, 0, 0)),
                  _resident((1, ATTN_W))],
        out_specs=pl.BlockSpec((None, tq, ATTN_W), lambda b, i: (b, i, 0)),
        out_shape=jax.ShapeDtypeStruct((bsz, seq, ATTN_W), BF16),
        scratch_shapes=[pltpu.VMEM((N_HEADS, 1, tq), F32),
                        pltpu.VMEM((N_HEADS, V_ROWS, tq), F32),
                        pltpu.VMEM((ATTN_W, tq), F32),
                        pltpu.VMEM((N_HEADS, 2, seq // N_KEY_CHUNKS // 2, tq), F32)],
        compiler_params=pltpu.CompilerParams(
            dimension_semantics=("parallel", "arbitrary"), vmem_limit_bytes=VMEM_LIMIT_BYTES),
        name="attention_bounded" if bounded else "attention_online",
    )(qt, k, vt, out_gain)


def _post_kernel(h_ref, hp_ref, hn_ref, oa_ref, oap_ref, oan_ref, oc_ref, ocp_ref, ocn_ref, p_ref,
                 wo_ref, fg_ref, wup_ref, fcw_ref, wdn_ref, wpg_ref, wpp_ref,
                 y_ref, hm_ref, a2_ref, ug_ref, uu_ref, acc_ref):
    i = pl.program_id(1)
    tm = h_ref.shape[0]
    lo, hi = F32_ROWS, F32_ROWS + tm

    def extended(main_ref, prev_ref, next_ref):
        prev = prev_ref[...].astype(F32)[BF16_ROWS - F32_ROWS:]
        nxt = next_ref[...].astype(F32)[:F32_ROWS]
        return jnp.concatenate([prev, main_ref[...].astype(F32), nxt], axis=0).astype(BF16)

    oa = extended(oa_ref, oap_ref, oan_ref)
    oc = extended(oc_ref, ocp_ref, ocn_ref)
    x = jnp.concatenate([hp_ref[...], h_ref[...], hn_ref[...]], axis=0)
    hm = (x + jnp.dot(oa, wo_ref[:ATTN_W, :], preferred_element_type=F32)
          + jnp.dot(oc, wo_ref[ATTN_W:, :], preferred_element_type=F32))
    hm_ref[...] = hm[lo:hi]
    keep = _edge_mask(hm.shape[0], tm, i == 0, i == pl.num_programs(1) - 1)
    a2_ref[...] = jnp.where(keep, _rms(hm, fg_ref[...]), 0.0).astype(BF16)
    acc_ref[...] = jnp.zeros_like(acc_ref)

    def ff_cols(which, c):
        return pl.ds(pl.multiple_of(which * D_FF + c * FF_CHUNK, FF_CHUNK), FF_CHUNK)

    def up_proj(c, slot):
        a2 = a2_ref[...]
        ug_ref[slot] = jnp.dot(a2, wup_ref[:, ff_cols(0, c)], preferred_element_type=F32)
        uu_ref[slot] = jnp.dot(a2, wup_ref[:, ff_cols(1, c)], preferred_element_type=F32)

    def act_down(c, slot):
        gate = _dwconv3(ug_ref.at[slot], fcw_ref[:, ff_cols(0, c)], 0, tm)
        lin = _dwconv3(uu_ref.at[slot], fcw_ref[:, ff_cols(1, c)], 0, tm)
        act = (gate * jax.nn.sigmoid(gate) * lin).astype(BF16)
        acc_ref[...] += jnp.dot(act, wdn_ref[c], preferred_element_type=F32)

    up_proj(0, 0)
    for c in range(N_FF_CHUNKS):
        if c + 1 < N_FF_CHUNKS:
            up_proj(c + 1, (c + 1) % 2)
        act_down(c, c % 2)

    h2 = hm_ref[...] + acc_ref[...]
    gate = jax.nn.sigmoid(jnp.dot(h2.astype(BF16), wpg_ref[...], preferred_element_type=F32))
    emb = jnp.dot(p_ref[...].astype(BF16), wpp_ref[...], preferred_element_type=F32)
    y_ref[...] = h2 + gate * emb


def _post(h, o_attn, o_conv, p, layer, lw):
    bsz, seq, _ = h.shape
    tm = min(TM_POST, seq)
    ext = tm + 2 * F32_ROWS
    return pl.pallas_call(
        _post_kernel,
        grid=(bsz, seq // tm),
        in_specs=(_halo_specs(tm, seq, D_MODEL, F32_ROWS)
                  + _halo_specs(tm, seq, ATTN_W, BF16_ROWS)
                  + _halo_specs(tm, seq, CONV_W, BF16_ROWS)
                  + [pl.BlockSpec((None, None, tm, PLE_DIM), lambda b, i: (layer, b, i, 0)),
                     _resident((ATTN_W + CONV_W, D_MODEL)), _resident((1, D_MODEL)),
                     _resident((D_MODEL, 2 * D_FF)),
                     _resident((3, 2 * D_FF)),
                     _resident((N_FF_CHUNKS, FF_CHUNK, D_MODEL)),
                     _resident((D_MODEL, D_MODEL)), _resident((PLE_DIM, D_MODEL))]),
        out_specs=pl.BlockSpec((None, tm, D_MODEL), lambda b, i: (b, i, 0)),
        out_shape=jax.ShapeDtypeStruct((bsz, seq, D_MODEL), F32),
        scratch_shapes=[pltpu.VMEM((tm, D_MODEL), F32),
                        pltpu.VMEM((ext, D_MODEL), BF16),
                        pltpu.VMEM((2, ext, FF_CHUNK), F32),
                        pltpu.VMEM((2, ext, FF_CHUNK), F32),
                        pltpu.VMEM((tm, D_MODEL), F32)],
        compiler_params=pltpu.CompilerParams(
            dimension_semantics=("parallel", "arbitrary"), vmem_limit_bytes=VMEM_LIMIT_BYTES),
        name="post",
    )(h, h, h, o_attn, o_attn, o_attn, o_conv, o_conv, o_conv, p,
      lw["w_out"], lw["ffn_norm_g"], lw["w_up"], lw["ffn_conv_w"], lw["w_down"],
      lw["w_ple_gate"], lw["w_ple_proj"])


def _rope_tables(seq):
    rows = seq // GRID_W
    row = jnp.repeat(jnp.arange(rows), GRID_W).astype(F32)
    col = jnp.tile(jnp.arange(GRID_W), rows).astype(F32)
    half = HEAD_DIM // 2
    freqs = ROPE_THETA ** (-jnp.arange(0, half, 2, dtype=F32) / half)
    ang_r = row[:, None] * freqs[None, :]
    ang_c = col[:, None] * freqs[None, :]
    ang = jnp.concatenate([ang_r, ang_r, ang_c, ang_c], axis=-1)
    cos, sin = jnp.cos(ang), jnp.sin(ang)
    first = (jnp.arange(HEAD_DIM) % (HEAD_DIM // 2)) < HEAD_DIM // 4
    sin_up = jnp.where(first, -sin, 0.0)
    sin_dn = jnp.where(first, 0.0, sin)
    widen = lambda t: jnp.tile(t, (1, LANES // HEAD_DIM))
    return widen(cos), widen(sin_up), widen(sin_dn)


def _layer_weights(i, attn_norm_g, w_in, q_norm_g, k_norm_g, mix_conv_w, attn_out_g, conv_out_g,
                   w_out, ffn_norm_g, w_up, ffn_conv_w, w_down, w_ple_gate, w_ple_proj):
    head = jnp.arange(ATTN_W) // HEAD_DIM
    q_scale = math.log2(math.e) / math.sqrt(HEAD_DIM)
    return {
        "attn_norm_g": attn_norm_g[i][None],
        "w_in": w_in[i].astype(BF16),
        "q_gain": jnp.tile(q_norm_g[i] * q_scale, N_HEADS)[None],
        "k_gain": jnp.tile(k_norm_g[i], N_KV_HEADS)[None],
        "scores_bounded": (HEAD_DIM * ROUNDING_SLACK * jnp.max(jnp.abs(q_norm_g[i] * q_scale))
                           * jnp.max(jnp.abs(k_norm_g[i])) <= SCORE_BOUND),
        "head_group": jnp.where(head[:, None] == head[None, :], 1.0 / HEAD_DIM, 0.0).astype(BF16),
        "mix_conv_w": mix_conv_w[i],
        "attn_out_g": attn_out_g[i][None],
        "conv_out_g": conv_out_g[i][None],
        "w_out": w_out[i].astype(BF16),
        "ffn_norm_g": ffn_norm_g[i][None],
        "w_up": w_up[i].astype(BF16),
        "ffn_conv_w": ffn_conv_w[i],
        "w_down": w_down[i].astype(BF16).reshape(N_FF_CHUNKS, FF_CHUNK, D_MODEL),
        "w_ple_gate": w_ple_gate[i].astype(BF16),
        "w_ple_proj": w_ple_proj[i].astype(BF16),
    }


def _trunk(x, p, layers):
    tables = _rope_tables(x.shape[1])
    h = x
    for i, lw in enumerate(layers):
        qt, k, vt, o_conv = _mixer_in(h, lw, tables)
        o_attn = lax.cond(lw["scores_bounded"],
                          functools.partial(_attention, bounded=True),
                          functools.partial(_attention, bounded=False),
                          qt, k, vt, lw["attn_out_g"])
        h = _post(h, o_attn, o_conv, p, i, lw)
    return h


def kernel(x_prompt, x_sample, p_prompt, p_sample, attn_norm_g, w_in, q_norm_g, k_norm_g, mix_conv_w,
           attn_out_g, conv_out_g, w_out, ffn_norm_g, w_up, ffn_conv_w, w_down, w_ple_gate, w_ple_proj):
    depth = w_in.shape[0]
    layers = [_layer_weights(i, attn_norm_g, w_in, q_norm_g, k_norm_g, mix_conv_w, attn_out_g,
                             conv_out_g, w_out, ffn_norm_g, w_up, ffn_conv_w, w_down, w_ple_gate,
                             w_ple_proj) for i in range(depth)]
    return (_trunk(x_prompt, p_prompt, layers), _trunk(x_sample, p_sample, layers))
```

```python
import functools
import math

import jax
import jax.numpy as jnp
from jax import lax
from jax.experimental import pallas as pl
from jax.experimental.pallas import tpu as pltpu

D_MODEL = 1024
GRID_W = 64
HEAD_DIM = 64
N_HEADS = 8
N_KV_HEADS = 2
Q_PER_KV = N_HEADS // N_KV_HEADS
ATTN_W = N_HEADS * HEAD_DIM
KV_W = N_KV_HEADS * HEAD_DIM
CONV_W = 512
D_IN = ATTN_W + 2 * KV_W + 3 * CONV_W
D_FF = 2816
PLE_DIM = 256
ROPE_THETA = 10000.0
EPS = 1e-6

LANES = 128
F32_ROWS = 8
BF16_ROWS = 16
VMEM_LIMIT_BYTES = 56 * 1024 * 1024

TM_IN = 1024
TQ = 512
N_KEY_CHUNKS = 2
SCORE_LOOKAHEAD = 2
V_ROWS = HEAD_DIM + BF16_ROWS
SCORE_BOUND = 50.0
ROUNDING_SLACK = 1.02
TM_POST = 512
FF_CHUNK = 256
N_FF_CHUNKS = D_FF // FF_CHUNK
assert N_FF_CHUNKS * FF_CHUNK == D_FF and N_FF_CHUNKS % 2 == 1

F32 = jnp.float32
BF16 = jnp.bfloat16


def _resident(shape):
    zeros = (0,) * len(shape)
    return pl.BlockSpec(shape, lambda *_: zeros, pipeline_mode=pl.Buffered(1))


def _rms(x, g):
    return x * lax.rsqrt(jnp.mean(x * x, axis=-1, keepdims=True) + EPS) * g


def _rope(x, cos, sin_up, sin_dn):
    cols = []
    for c in range(x.shape[1] // LANES):
        xc = x[:, c * LANES:(c + 1) * LANES]
        up = pltpu.roll(xc, LANES - HEAD_DIM // 4, 1)
        dn = pltpu.roll(xc, HEAD_DIM // 4, 1)
        cols.append(xc * cos + up * sin_up + dn * sin_dn)
    return jnp.concatenate(cols, axis=1) if len(cols) > 1 else cols[0]


def _edge_mask(rows, tm, is_first, is_last):
    r = lax.broadcasted_iota(jnp.int32, (rows, 1), 0)
    outside = jnp.logical_or(jnp.logical_and(r < F32_ROWS, is_first),
                             jnp.logical_and(r >= F32_ROWS + tm, is_last))
    return jnp.logical_not(outside)


def _dwconv3(ref, w, r0, nrows):
    x = ref[r0:r0 + nrows + 2 * F32_ROWS, :]
    lo = F32_ROWS
    return (pltpu.roll(x, 1, 0)[lo:lo + nrows] * w[0:1, :] + x[lo:lo + nrows] * w[1:2, :]
            + pltpu.roll(x, x.shape[0] - 1, 0)[lo:lo + nrows] * w[2:3, :])


def _mixer_in_kernel(h_ref, hp_ref, hn_ref, ng_ref, w_ref, qg_ref, kg_ref, cos_ref, su_ref, sd_ref,
                     grp_ref, cw_ref, cg_ref,
                     qt_ref, k_ref, vt_ref, oc_ref, cu_ref):
    i = pl.program_id(1)
    tm = h_ref.shape[0]
    lo, hi = F32_ROWS, F32_ROWS + tm
    x = jnp.concatenate([hp_ref[...], h_ref[...], hn_ref[...]], axis=0)
    a = _rms(x, ng_ref[...]).astype(BF16)

    def proj(c0, c1):
        return jnp.dot(a, w_ref[:, c0:c1], preferred_element_type=F32)

    cos, s_up, s_dn = cos_ref[...], su_ref[...], sd_ref[...]

    def head_norm(z, g):
        w = z.shape[1]
        ms = jnp.dot((z * z).astype(BF16), grp_ref[:w, :w], preferred_element_type=F32)
        return z * lax.rsqrt(ms + EPS) * g

    c0 = ATTN_W + 2 * KV_W
    zq = proj(0, ATTN_W)[lo:hi]
    cu = proj(c0 + CONV_W, c0 + 2 * CONV_W) * proj(c0 + 2 * CONV_W, c0 + 3 * CONV_W)
    keep = _edge_mask(cu.shape[0], tm, i == 0, i == pl.num_programs(1) - 1)
    cu_ref[...] = jnp.where(keep, cu, 0.0)

    qt_ref[...] = _rope(head_norm(zq, qg_ref[...]), cos, s_up, s_dn).T.astype(BF16)
    conv = _dwconv3(cu_ref, cw_ref[...], 0, tm)

    zkv = proj(ATTN_W, ATTN_W + 2 * KV_W)[lo:hi]
    k = _rope(head_norm(zkv[:, :KV_W], kg_ref[...]), cos, s_up, s_dn).astype(BF16)
    for g in range(N_KV_HEADS):
        k_ref[g] = k[:, g * HEAD_DIM:(g + 1) * HEAD_DIM]
    vt = zkv[:, KV_W:].T.astype(BF16)
    for g in range(N_KV_HEADS):
        vt_ref[g * V_ROWS:g * V_ROWS + HEAD_DIM, :] = vt[g * HEAD_DIM:(g + 1) * HEAD_DIM]
        vt_ref[g * V_ROWS + HEAD_DIM:(g + 1) * V_ROWS, :] = jnp.ones((BF16_ROWS, tm), BF16)

    gb = proj(c0, c0 + CONV_W)[lo:hi]
    oc_ref[...] = _rms(gb * conv, cg_ref[...]).astype(BF16)


def _halo_specs(tm, seq, width, rows):
    per = tm // rows
    last = seq // rows - 1
    return [
        pl.BlockSpec((None, tm, width), lambda b, i: (b, i, 0)),
        pl.BlockSpec((None, rows, width), lambda b, i: (b, jnp.maximum(i * per - 1, 0), 0)),
        pl.BlockSpec((None, rows, width), lambda b, i: (b, jnp.minimum((i + 1) * per, last), 0)),
    ]


def _mixer_in(h, lw, tables):
    bsz, seq, _ = h.shape
    tm = min(TM_IN, seq)
    cos, s_up, s_dn = tables
    tab_spec = pl.BlockSpec((tm, LANES), lambda b, i: (i, 0))
    tok = lambda w: pl.BlockSpec((None, tm, w), lambda b, i: (b, i, 0))
    tok_t = lambda w: pl.BlockSpec((None, w, tm), lambda b, i: (b, 0, i))
    return pl.pallas_call(
        _mixer_in_kernel,
        grid=(bsz, seq // tm),
        in_specs=_halo_specs(tm, seq, D_MODEL, F32_ROWS) + [
            _resident((1, D_MODEL)), _resident((D_MODEL, D_IN)),
            _resident((1, ATTN_W)), _resident((1, KV_W)),
            tab_spec, tab_spec, tab_spec,
            _resident((ATTN_W, ATTN_W)), _resident((3, CONV_W)), _resident((1, CONV_W)),
        ],
        out_specs=[tok_t(ATTN_W),
                   pl.BlockSpec((None, N_KV_HEADS, tm, HEAD_DIM), lambda b, i: (b, 0, i, 0)),
                   tok_t(N_KV_HEADS * V_ROWS), tok(CONV_W)],
        out_shape=[jax.ShapeDtypeStruct((bsz, ATTN_W, seq), BF16),
                   jax.ShapeDtypeStruct((bsz, N_KV_HEADS, seq, HEAD_DIM), BF16),
                   jax.ShapeDtypeStruct((bsz, N_KV_HEADS * V_ROWS, seq), BF16),
                   jax.ShapeDtypeStruct((bsz, seq, CONV_W), BF16)],
        scratch_shapes=[pltpu.VMEM((tm + 2 * F32_ROWS, CONV_W), F32)],
        compiler_params=pltpu.CompilerParams(
            dimension_semantics=("parallel", "arbitrary"), vmem_limit_bytes=VMEM_LIMIT_BYTES),
        name="mixer_in",
    )(h, h, h, lw["attn_norm_g"], lw["w_in"], lw["q_gain"], lw["k_gain"], cos, s_up, s_dn,
      lw["head_group"], lw["mix_conv_w"], lw["conv_out_g"])


def _attention_kernel(qt_ref, k_ref, vt_ref, og_ref, o_ref, m_ref, acc_ref, ot_ref, stage_ref, *, bounded):
    seq = k_ref.shape[1]
    n_chunks = N_KEY_CHUNKS
    tk = seq // n_chunks
    half = tk // 2

    if not bounded:
        m_ref[...] = jnp.full_like(m_ref, -jnp.inf)
    acc_ref[...] = jnp.zeros_like(acc_ref)

    def stage_scores(c, h):
        start = pl.multiple_of(c * tk, tk)
        for j in range(2):
            stage_ref[h, j] = jnp.dot(k_ref[h // Q_PER_KV, pl.ds(start + j * half, half), :],
                                      qt_ref[h * HEAD_DIM:(h + 1) * HEAD_DIM, :],
                                      preferred_element_type=F32)

    for h in range(SCORE_LOOKAHEAD):
        stage_scores(0, h)

    def chunk(c, wrap):
        start = pl.multiple_of(c * tk, tk)
        for h in range(N_HEADS):
            g = h // Q_PER_KV
            ahead = h + SCORE_LOOKAHEAD
            if ahead < N_HEADS:
                stage_scores(c, ahead)
            elif wrap:
                stage_scores(c + 1, ahead - N_HEADS)
            vt = vt_ref[g * V_ROWS:(g + 1) * V_ROWS, pl.ds(start, tk)]
            if bounded:
                pb = jnp.concatenate([jnp.exp2(stage_ref[h, j]).astype(BF16) for j in range(2)], axis=0)
                acc_ref[h] += jnp.dot(vt, pb, preferred_element_type=F32)
                continue
            m_old = m_ref[h]
            m_new = jnp.maximum(m_old, jnp.maximum(jnp.max(stage_ref[h, 0], axis=0, keepdims=True),
                                                   jnp.max(stage_ref[h, 1], axis=0, keepdims=True)))
            alpha = jnp.exp2(m_old - m_new)
            pb = jnp.concatenate([jnp.exp2(stage_ref[h, j] - m_new).astype(BF16) for j in range(2)],
                                 axis=0)
            acc_ref[h] = alpha * acc_ref[h] + jnp.dot(vt, pb, preferred_element_type=F32)
            m_ref[h] = m_new

    def wrapped(c, carry):
        chunk(c, True)
        return carry

    lax.fori_loop(0, n_chunks - 1, wrapped, 0)
    chunk(n_chunks - 1, False)

    for h in range(N_HEADS):
        ot_ref[h * HEAD_DIM:(h + 1) * HEAD_DIM, :] = (
            acc_ref[h, :HEAD_DIM, :] / acc_ref[h, HEAD_DIM:HEAD_DIM + 1, :])
    o_ref[...] = _rms(ot_ref[...].T, og_ref[...]).astype(BF16)


def _attention(qt, k, vt, out_gain, bounded):
    bsz, _, seq, _ = k.shape
    tq = min(TQ, seq)
    return pl.pallas_call(
        functools.partial(_attention_kernel, bounded=bounded),
        grid=(bsz, seq // tq),
        in_specs=[pl.BlockSpec((None, ATTN_W, tq), lambda b, i: (b, 0, i)),
                  pl.BlockSpec((None, N_KV_HEADS, seq, HEAD_DIM), lambda b, i: (b, 0, 0, 0)),
                  pl.BlockSpec((None, N_KV_HEADS * V_ROWS, seq), lambda b, i: (b, 0, 0)),
                  _resident((1, ATTN_W))],
        out_specs=pl.BlockSpec((None, tq, ATTN_W), lambda b, i: (b, i, 0)),
        out_shape=jax.ShapeDtypeStruct((bsz, seq, ATTN_W), BF16),
        scratch_shapes=[pltpu.VMEM((N_HEADS, 1, tq), F32),
                        pltpu.VMEM((N_HEADS, V_ROWS, tq), F32),
                        pltpu.VMEM((ATTN_W, tq), F32),
                        pltpu.VMEM((N_HEADS, 2, seq // N_KEY_CHUNKS // 2, tq), F32)],
        compiler_params=pltpu.CompilerParams(
            dimension_semantics=("parallel", "arbitrary"), vmem_limit_bytes=VMEM_LIMIT_BYTES),
        name="attention_bounded" if bounded else "attention_online",
    )(qt, k, vt, out_gain)


def _post_kernel(h_ref, hp_ref, hn_ref, oa_ref, oap_ref, oan_ref, oc_ref, ocp_ref, ocn_ref, p_ref,
                 wo_ref, fg_ref, wup_ref, fcw_ref, wdn_ref, wpg_ref, wpp_ref,
                 y_ref, hm_ref, a2_ref, ug_ref, uu_ref, acc_ref):
    i = pl.program_id(1)
    tm = h_ref.shape[0]
    lo, hi = F32_ROWS, F32_ROWS + tm

    def extended(main_ref, prev_ref, next_ref):
        prev = prev_ref[...].astype(F32)[BF16_ROWS - F32_ROWS:]
        nxt = next_ref[...].astype(F32)[:F32_ROWS]
        return jnp.concatenate([prev, main_ref[...].astype(F32), nxt], axis=0).astype(BF16)

    oa = extended(oa_ref, oap_ref, oan_ref)
    oc = extended(oc_ref, ocp_ref, ocn_ref)
    x = jnp.concatenate([hp_ref[...], h_ref[...], hn_ref[...]], axis=0)
    hm = (x + jnp.dot(oa, wo_ref[:ATTN_W, :], preferred_element_type=F32)
          + jnp.dot(oc, wo_ref[ATTN_W:, :], preferred_element_type=F32))
    hm_ref[...] = hm[lo:hi]
    keep = _edge_mask(hm.shape[0], tm, i == 0, i == pl.num_programs(1) - 1)
    a2_ref[...] = jnp.where(keep, _rms(hm, fg_ref[...]), 0.0).astype(BF16)
    acc_ref[...] = jnp.zeros_like(acc_ref)

    def ff_cols(which, c):
        return pl.ds(pl.multiple_of(which * D_FF + c * FF_CHUNK, FF_CHUNK), FF_CHUNK)

    def up_proj(c, slot):
        a2 = a2_ref[...]
        ug_ref[slot] = jnp.dot(a2, wup_ref[:, ff_cols(0, c)], preferred_element_type=F32)
        uu_ref[slot] = jnp.dot(a2, wup_ref[:, ff_cols(1, c)], preferred_element_type=F32)

    def act_down(c, slot):
        gate = _dwconv3(ug_ref.at[slot], fcw_ref[:, ff_cols(0, c)], 0, tm)
        lin = _dwconv3(uu_ref.at[slot], fcw_ref[:, ff_cols(1, c)], 0, tm)
        act = (gate * jax.nn.sigmoid(gate) * lin).astype(BF16)
        acc_ref[...] += jnp.dot(act, wdn_ref[c], preferred_element_type=F32)

    up_proj(0, 0)

    def ff_pair(i, carry):
        c = 2 * i
        up_proj(c + 1, 1)
        act_down(c, 0)
        up_proj(c + 2, 0)
        act_down(c + 1, 1)
        return carry

    lax.fori_loop(0, (N_FF_CHUNKS - 1) // 2, ff_pair, 0)
    act_down(N_FF_CHUNKS - 1, 0)

    h2 = hm_ref[...] + acc_ref[...]
    gate = jax.nn.sigmoid(jnp.dot(h2.astype(BF16), wpg_ref[...], preferred_element_type=F32))
    emb = jnp.dot(p_ref[...].astype(BF16), wpp_ref[...], preferred_element_type=F32)
    y_ref[...] = h2 + gate * emb


def _post(h, o_attn, o_conv, p, layer, lw):
    bsz, seq, _ = h.shape
    tm = min(TM_POST, seq)
    ext = tm + 2 * F32_ROWS
    return pl.pallas_call(
        _post_kernel,
        grid=(bsz, seq // tm),
        in_specs=(_halo_specs(tm, seq, D_MODEL, F32_ROWS)
                  + _halo_specs(tm, seq, ATTN_W, BF16_ROWS)
                  + _halo_specs(tm, seq, CONV_W, BF16_ROWS)
                  + [pl.BlockSpec((None, None, tm, PLE_DIM), lambda b, i: (layer, b, i, 0)),
                     _resident((ATTN_W + CONV_W, D_MODEL)), _resident((1, D_MODEL)),
                     _resident((D_MODEL, 2 * D_FF)),
                     _resident((3, 2 * D_FF)),
                     _resident((N_FF_CHUNKS, FF_CHUNK, D_MODEL)),
                     _resident((D_MODEL, D_MODEL)), _resident((PLE_DIM, D_MODEL))]),
        out_specs=pl.BlockSpec((None, tm, D_MODEL), lambda b, i: (b, i, 0)),
        out_shape=jax.ShapeDtypeStruct((bsz, seq, D_MODEL), F32),
        scratch_shapes=[pltpu.VMEM((tm, D_MODEL), F32),
                        pltpu.VMEM((ext, D_MODEL), BF16),
                        pltpu.VMEM((2, ext, FF_CHUNK), F32),
                        pltpu.VMEM((2, ext, FF_CHUNK), F32),
                        pltpu.VMEM((tm, D_MODEL), F32)],
        compiler_params=pltpu.CompilerParams(
            dimension_semantics=("parallel", "arbitrary"), vmem_limit_bytes=VMEM_LIMIT_BYTES),
        name="post",
    )(h, h, h, o_attn, o_attn, o_attn, o_conv, o_conv, o_conv, p,
      lw["w_out"], lw["ffn_norm_g"], lw["w_up"], lw["ffn_conv_w"], lw["w_down"],
      lw["w_ple_gate"], lw["w_ple_proj"])


def _rope_tables(seq):
    rows = seq // GRID_W
    row = jnp.repeat(jnp.arange(rows), GRID_W).astype(F32)
    col = jnp.tile(jnp.arange(GRID_W), rows).astype(F32)
    half = HEAD_DIM // 2
    freqs = ROPE_THETA ** (-jnp.arange(0, half, 2, dtype=F32) / half)
    ang_r = row[:, None] * freqs[None, :]
    ang_c = col[:, None] * freqs[None, :]
    ang = jnp.concatenate([ang_r, ang_r, ang_c, ang_c], axis=-1)
    cos, sin = jnp.cos(ang), jnp.sin(ang)
    first = (jnp.arange(HEAD_DIM) % (HEAD_DIM // 2)) < HEAD_DIM // 4
    sin_up = jnp.where(first, -sin, 0.0)
    sin_dn = jnp.where(first, 0.0, sin)
    widen = lambda t: jnp.tile(t, (1, LANES // HEAD_DIM))
    return widen(cos), widen(sin_up), widen(sin_dn)


def _layer_weights(i, attn_norm_g, w_in, q_norm_g, k_norm_g, mix_conv_w, attn_out_g, conv_out_g,
                   w_out, ffn_norm_g, w_up, ffn_conv_w, w_down, w_ple_gate, w_ple_proj):
    head = jnp.arange(ATTN_W) // HEAD_DIM
    q_scale = math.log2(math.e) / math.sqrt(HEAD_DIM)
    return {
        "attn_norm_g": attn_norm_g[i][None],
        "w_in": w_in[i].astype(BF16),
        "q_gain": jnp.tile(q_norm_g[i] * q_scale, N_HEADS)[None],
        "k_gain": jnp.tile(k_norm_g[i], N_KV_HEADS)[None],
        "scores_bounded": (HEAD_DIM * ROUNDING_SLACK * jnp.max(jnp.abs(q_norm_g[i] * q_scale))
                           * jnp.max(jnp.abs(k_norm_g[i])) <= SCORE_BOUND),
        "head_group": jnp.where(head[:, None] == head[None, :], 1.0 / HEAD_DIM, 0.0).astype(BF16),
        "mix_conv_w": mix_conv_w[i],
        "attn_out_g": attn_out_g[i][None],
        "conv_out_g": conv_out_g[i][None],
        "w_out": w_out[i].astype(BF16),
        "ffn_norm_g": ffn_norm_g[i][None],
        "w_up": w_up[i].astype(BF16),
        "ffn_conv_w": ffn_conv_w[i],
        "w_down": w_down[i].astype(BF16).reshape(N_FF_CHUNKS, FF_CHUNK, D_MODEL),
        "w_ple_gate": w_ple_gate[i].astype(BF16),
        "w_ple_proj": w_ple_proj[i].astype(BF16),
    }


def _trunk(x, p, layers):
    tables = _rope_tables(x.shape[1])
    h = x
    for i, lw in enumerate(layers):
        qt, k, vt, o_conv = _mixer_in(h, lw, tables)
        o_attn = lax.cond(lw["scores_bounded"],
                          functools.partial(_attention, bounded=True),
                          functools.partial(_attention, bounded=False),
                          qt, k, vt, lw["attn_out_g"])
        h = _post(h, o_attn, o_conv, p, i, lw)
    return h


def kernel(x_prompt, x_sample, p_prompt, p_sample, attn_norm_g, w_in, q_norm_g, k_norm_g, mix_conv_w,
           attn_out_g, conv_out_g, w_out, ffn_norm_g, w_up, ffn_conv_w, w_down, w_ple_gate, w_ple_proj):
    depth = w_in.shape[0]
    layers = [_layer_weights(i, attn_norm_g, w_in, q_norm_g, k_norm_g, mix_conv_w, attn_out_g,
                             conv_out_g, w_out, ffn_norm_g, w_up, ffn_conv_w, w_down, w_ple_gate,
                             w_ple_proj) for i in range(depth)]
    return (_trunk(x_prompt, p_prompt, layers), _trunk(x_sample, p_sample, layers))
```

```python
import functools
import math

import jax
import jax.numpy as jnp
from jax import lax
from jax.experimental import pallas as pl
from jax.experimental.pallas import tpu as pltpu

D_MODEL = 1024
GRID_W = 64
HEAD_DIM = 64
N_HEADS = 8
N_KV_HEADS = 2
Q_PER_KV = N_HEADS // N_KV_HEADS
ATTN_W = N_HEADS * HEAD_DIM
KV_W = N_KV_HEADS * HEAD_DIM
CONV_W = 512
D_IN = ATTN_W + 2 * KV_W + 3 * CONV_W
D_FF = 2816
PLE_DIM = 256
ROPE_THETA = 10000.0
EPS = 1e-6

LANES = 128
F32_ROWS = 8
BF16_ROWS = 16
VMEM_LIMIT_BYTES = 56 * 1024 * 1024

TM_IN = 1024
TQ = 512
N_KEY_CHUNKS = 2
SCORE_LOOKAHEAD = 2
V_ROWS = HEAD_DIM + BF16_ROWS
SCORE_BOUND = 50.0
ROUNDING_SLACK = 1.02
TM_POST = 512
FF_CHUNK = 256
N_FF_CHUNKS = D_FF // FF_CHUNK
assert N_FF_CHUNKS * FF_CHUNK == D_FF and N_FF_CHUNKS % 2 == 1

F32 = jnp.float32
BF16 = jnp.bfloat16


def _resident(shape):
    zeros = (0,) * len(shape)
    return pl.BlockSpec(shape, lambda *_: zeros, pipeline_mode=pl.Buffered(1))


def _rms(x, g):
    return x * lax.rsqrt(jnp.mean(x * x, axis=-1, keepdims=True) + EPS) * g


def _rope(x, cos, sin_up, sin_dn):
    cols = []
    for c in range(x.shape[1] // LANES):
        xc = x[:, c * LANES:(c + 1) * LANES]
        up = pltpu.roll(xc, LANES - HEAD_DIM // 4, 1)
        dn = pltpu.roll(xc, HEAD_DIM // 4, 1)
        cols.append(xc * cos + up * sin_up + dn * sin_dn)
    return jnp.concatenate(cols, axis=1) if len(cols) > 1 else cols[0]


def _edge_mask(rows, tm, is_first, is_last):
    r = lax.broadcasted_iota(jnp.int32, (rows, 1), 0)
    outside = jnp.logical_or(jnp.logical_and(r < F32_ROWS, is_first),
                             jnp.logical_and(r >= F32_ROWS + tm, is_last))
    return jnp.logical_not(outside)


def _dwconv3(ref, w, r0, nrows):
    x = ref[r0:r0 + nrows + 2 * F32_ROWS, :]
    lo = F32_ROWS
    return (pltpu.roll(x, 1, 0)[lo:lo + nrows] * w[0:1, :] + x[lo:lo + nrows] * w[1:2, :]
            + pltpu.roll(x, x.shape[0] - 1, 0)[lo:lo + nrows] * w[2:3, :])


def _mixer_in_kernel(h_ref, hp_ref, hn_ref, ng_ref, w_ref, qg_ref, kg_ref, cos_ref, su_ref, sd_ref,
                     grp_ref, cw_ref, cg_ref,
                     qt_ref, k_ref, vt_ref, oc_ref, cu_ref):
    i = pl.program_id(1)
    tm = h_ref.shape[0]
    lo, hi = F32_ROWS, F32_ROWS + tm
    x = jnp.concatenate([hp_ref[...], h_ref[...], hn_ref[...]], axis=0)
    a = _rms(x, ng_ref[...]).astype(BF16)

    def proj(c0, c1):
        return jnp.dot(a, w_ref[:, c0:c1], preferred_element_type=F32)

    cos, s_up, s_dn = cos_ref[...], su_ref[...], sd_ref[...]

    def head_norm(z, g):
        w = z.shape[1]
        ms = jnp.dot((z * z).astype(BF16), grp_ref[:w, :w], preferred_element_type=F32)
        return z * lax.rsqrt(ms + EPS) * g

    c0 = ATTN_W + 2 * KV_W
    zq = proj(0, ATTN_W)[lo:hi]
    cu = proj(c0 + CONV_W, c0 + 2 * CONV_W) * proj(c0 + 2 * CONV_W, c0 + 3 * CONV_W)
    keep = _edge_mask(cu.shape[0], tm, i == 0, i == pl.num_programs(1) - 1)
    cu_ref[...] = jnp.where(keep, cu, 0.0)

    qt_ref[...] = _rope(head_norm(zq, qg_ref[...]), cos, s_up, s_dn).T.astype(BF16)
    conv = _dwconv3(cu_ref, cw_ref[...], 0, tm)

    zkv = proj(ATTN_W, ATTN_W + 2 * KV_W)[lo:hi]
    k = _rope(head_norm(zkv[:, :KV_W], kg_ref[...]), cos, s_up, s_dn).astype(BF16)
    for g in range(N_KV_HEADS):
        k_ref[g] = k[:, g * HEAD_DIM:(g + 1) * HEAD_DIM]
    vt = zkv[:, KV_W:].T.astype(BF16)
    for g in range(N_KV_HEADS):
        vt_ref[g * V_ROWS:g * V_ROWS + HEAD_DIM, :] = vt[g * HEAD_DIM:(g + 1) * HEAD_DIM]
        vt_ref[g * V_ROWS + HEAD_DIM:(g + 1) * V_ROWS, :] = jnp.ones((BF16_ROWS, tm), BF16)

    gb = proj(c0, c0 + CONV_W)[lo:hi]
    oc_ref[...] = _rms(gb * conv, cg_ref[...]).astype(BF16)


def _halo_specs(tm, seq, width, rows):
    per = tm // rows
    last = seq // rows - 1
    return [
        pl.BlockSpec((None, tm, width), lambda b, i: (b, i, 0)),
        pl.BlockSpec((None, rows, width), lambda b, i: (b, jnp.maximum(i * per - 1, 0), 0)),
        pl.BlockSpec((None, rows, width), lambda b, i: (b, jnp.minimum((i + 1) * per, last), 0)),
    ]


def _mixer_in(h, lw, tables):
    bsz, seq, _ = h.shape
    tm = min(TM_IN, seq)
    cos, s_up, s_dn = tables
    tab_spec = pl.BlockSpec((tm, LANES), lambda b, i: (i, 0))
    tok = lambda w: pl.BlockSpec((None, tm, w), lambda b, i: (b, i, 0))
    tok_t = lambda w: pl.BlockSpec((None, w, tm), lambda b, i: (b, 0, i))
    return pl.pallas_call(
        _mixer_in_kernel,
        grid=(bsz, seq // tm),
        in_specs=_halo_specs(tm, seq, D_MODEL, F32_ROWS) + [
            _resident((1, D_MODEL)), _resident((D_MODEL, D_IN)),
            _resident((1, ATTN_W)), _resident((1, KV_W)),
            tab_spec, tab_spec, tab_spec,
            _resident((ATTN_W, ATTN_W)), _resident((3, CONV_W)), _resident((1, CONV_W)),
        ],
        out_specs=[tok_t(ATTN_W),
                   pl.BlockSpec((None, N_KV_HEADS, tm, HEAD_DIM), lambda b, i: (b, 0, i, 0)),
                   tok_t(N_KV_HEADS * V_ROWS), tok(CONV_W)],
        out_shape=[jax.ShapeDtypeStruct((bsz, ATTN_W, seq), BF16),
                   jax.ShapeDtypeStruct((bsz, N_KV_HEADS, seq, HEAD_DIM), BF16),
                   jax.ShapeDtypeStruct((bsz, N_KV_HEADS * V_ROWS, seq), BF16),
                   jax.ShapeDtypeStruct((bsz, seq, CONV_W), BF16)],
        scratch_shapes=[pltpu.VMEM((tm + 2 * F32_ROWS, CONV_W), F32)],
        compiler_params=pltpu.CompilerParams(
            dimension_semantics=("parallel", "arbitrary"), vmem_limit_bytes=VMEM_LIMIT_BYTES),
        name="mixer_in",
    )(h, h, h, lw["attn_norm_g"], lw["w_in"], lw["q_gain"], lw["k_gain"], cos, s_up, s_dn,
      lw["head_group"], lw["mix_conv_w"], lw["conv_out_g"])


def _attention_kernel(qt_ref, k_ref, vt_ref, og_ref, o_ref, m_ref, acc_ref, ot_ref, stage_ref, *, bounded):
    seq = k_ref.shape[1]
    n_chunks = N_KEY_CHUNKS
    tk = seq // n_chunks
    half = tk // 2

    if not bounded:
        m_ref[...] = jnp.full_like(m_ref, -jnp.inf)
    acc_ref[...] = jnp.zeros_like(acc_ref)

    def stage_scores(c, h):
        start = pl.multiple_of(c * tk, tk)
        for j in range(2):
            stage_ref[h, j] = jnp.dot(k_ref[h // Q_PER_KV, pl.ds(start + j * half, half), :],
                                      qt_ref[h * HEAD_DIM:(h + 1) * HEAD_DIM, :],
                                      preferred_element_type=F32)

    for h in range(SCORE_LOOKAHEAD):
        stage_scores(0, h)

    def chunk(c, wrap):
        start = pl.multiple_of(c * tk, tk)
        for h in range(N_HEADS):
            g = h // Q_PER_KV
            ahead = h + SCORE_LOOKAHEAD
            if ahead < N_HEADS:
                stage_scores(c, ahead)
            elif wrap:
                stage_scores(c + 1, ahead - N_HEADS)
            vt = vt_ref[g * V_ROWS:(g + 1) * V_ROWS, pl.ds(start, tk)]
            if bounded:
                pb = jnp.concatenate([jnp.exp2(stage_ref[h, j]).astype(BF16) for j in range(2)], axis=0)
                acc_ref[h] += jnp.dot(vt, pb, preferred_element_type=F32)
                continue
            m_old = m_ref[h]
            m_new = jnp.maximum(m_old, jnp.maximum(jnp.max(stage_ref[h, 0], axis=0, keepdims=True),
                                                   jnp.max(stage_ref[h, 1], axis=0, keepdims=True)))
            alpha = jnp.exp2(m_old - m_new)
            pb = jnp.concatenate([jnp.exp2(stage_ref[h, j] - m_new).astype(BF16) for j in range(2)],
                                 axis=0)
            acc_ref[h] = alpha * acc_ref[h] + jnp.dot(vt, pb, preferred_element_type=F32)
            m_ref[h] = m_new

    def wrapped(c, carry):
        chunk(c, True)
        return carry

    lax.fori_loop(0, n_chunks - 1, wrapped, 0)
    chunk(n_chunks - 1, False)

    for h in range(N_HEADS):
        ot_ref[h * HEAD_DIM:(h + 1) * HEAD_DIM, :] = (
            acc_ref[h, :HEAD_DIM, :] / acc_ref[h, HEAD_DIM:HEAD_DIM + 1, :])
    o_ref[...] = _rms(ot_ref[...].T, og_ref[...]).astype(BF16)


def _attention(qt, k, vt, out_gain, bounded):
    bsz, _, seq, _ = k.shape
    tq = min(TQ, seq)
    return pl.pallas_call(
        functools.partial(_attention_kernel, bounded=bounded),
        grid=(bsz, seq // tq),
        in_specs=[pl.BlockSpec((None, ATTN_W, tq), lambda b, i: (b, 0, i)),
                  pl.BlockSpec((None, N_KV_HEADS, seq, HEAD_DIM), lambda b, i: (b, 0, 0, 0)),
                  pl.BlockSpec((None, N_KV_HEADS * V_ROWS, seq), lambda b, i: (b, 0, 0)),
                  _resident((1, ATTN_W))],
        out_specs=pl.BlockSpec((None, tq, ATTN_W), lambda b, i: (b, i, 0)),
        out_shape=jax.ShapeDtypeStruct((bsz, seq, ATTN_W), BF16),
        scratch_shapes=[pltpu.VMEM((N_HEADS, 1, tq), F32),
                        pltpu.VMEM((N_HEADS, V_ROWS, tq), F32),
                        pltpu.VMEM((ATTN_W, tq), F32),
                        pltpu.VMEM((N_HEADS, 2, seq // N_KEY_CHUNKS // 2, tq), F32)],
        compiler_params=pltpu.CompilerParams(
            dimension_semantics=("parallel", "arbitrary"), vmem_limit_bytes=VMEM_LIMIT_BYTES),
        name="attention_bounded" if bounded else "attention_online",
    )(qt, k, vt, out_gain)


def _post_kernel(h_ref, hp_ref, hn_ref, oa_ref, oap_ref, oan_ref, oc_ref, ocp_ref, ocn_ref, p_ref,
                 wo_ref, fg_ref, wup_ref, fcw_ref, wdn_ref, wpg_ref, wpp_ref,
                 y_ref, hm_ref, a2_ref, ug_ref, uu_ref, acc_ref):
    i = pl.program_id(1)
    tm = h_ref.shape[0]
    lo, hi = F32_ROWS, F32_ROWS + tm

    def extended(main_ref, prev_ref, next_ref):
        prev = prev_ref[...].astype(F32)[BF16_ROWS - F32_ROWS:]
        nxt = next_ref[...].astype(F32)[:F32_ROWS]
        return jnp.concatenate([prev, main_ref[...].astype(F32), nxt], axis=0).astype(BF16)

    oa = extended(oa_ref, oap_ref, oan_ref)
    oc = extended(oc_ref, ocp_ref, ocn_ref)
    x = jnp.concatenate([hp_ref[...], h_ref[...], hn_ref[...]], axis=0)
    hm = (x + jnp.dot(oa, wo_ref[:ATTN_W, :], preferred_element_type=F32)
          + jnp.dot(oc, wo_ref[ATTN_W:, :], preferred_element_type=F32))
    hm_ref[...] = hm[lo:hi]
    keep = _edge_mask(hm.shape[0], tm, i == 0, i == pl.num_programs(1) - 1)
    a2_ref[...] = jnp.where(keep, _rms(hm, fg_ref[...]), 0.0).astype(BF16)
    acc_ref[...] = jnp.zeros_like(acc_ref)

    def ff_cols(which, c):
        return pl.ds(pl.multiple_of(which * D_FF + c * FF_CHUNK, FF_CHUNK), FF_CHUNK)

    def up_proj(c, slot):
        a2 = a2_ref[...]
        ug_ref[slot] = jnp.dot(a2, wup_ref[:, ff_cols(0, c)], preferred_element_type=F32)
        uu_ref[slot] = jnp.dot(a2, wup_ref[:, ff_cols(1, c)], preferred_element_type=F32)

    def act(c, slot):
        gate = _dwconv3(ug_ref.at[slot], fcw_ref[:, ff_cols(0, c)], 0, tm)
        lin = _dwconv3(uu_ref.at[slot], fcw_ref[:, ff_cols(1, c)], 0, tm)
        return (gate * jax.nn.sigmoid(gate) * lin).astype(BF16)

    up_proj(0, 0)

    def ff_pair(i, carry):
        c = 2 * i
        up_proj(c + 1, 1)
        a0 = act(c, 0)
        up_proj(c + 2, 0)
        a1 = act(c + 1, 1)
        rows = pl.ds(pl.multiple_of(c * FF_CHUNK, 2 * FF_CHUNK), 2 * FF_CHUNK)
        acc_ref[...] += jnp.dot(jnp.concatenate([a0, a1], axis=1), wdn_ref[rows, :],
                                preferred_element_type=F32)
        return carry

    lax.fori_loop(0, (N_FF_CHUNKS - 1) // 2, ff_pair, 0)
    last = N_FF_CHUNKS - 1
    acc_ref[...] += jnp.dot(act(last, 0), wdn_ref[last * FF_CHUNK:, :], preferred_element_type=F32)

    h2 = hm_ref[...] + acc_ref[...]
    gate = jax.nn.sigmoid(jnp.dot(h2.astype(BF16), wpg_ref[...], preferred_element_type=F32))
    emb = jnp.dot(p_ref[...].astype(BF16), wpp_ref[...], preferred_element_type=F32)
    y_ref[...] = h2 + gate * emb


def _post(h, o_attn, o_conv, p, layer, lw):
    bsz, seq, _ = h.shape
    tm = min(TM_POST, seq)
    ext = tm + 2 * F32_ROWS
    return pl.pallas_call(
        _post_kernel,
        grid=(bsz, seq // tm),
        in_specs=(_halo_specs(tm, seq, D_MODEL, F32_ROWS)
                  + _halo_specs(tm, seq, ATTN_W, BF16_ROWS)
                  + _halo_specs(tm, seq, CONV_W, BF16_ROWS)
                  + [pl.BlockSpec((None, None, tm, PLE_DIM), lambda b, i: (layer, b, i, 0)),
                     _resident((ATTN_W + CONV_W, D_MODEL)), _resident((1, D_MODEL)),
                     _resident((D_MODEL, 2 * D_FF)),
                     _resident((3, 2 * D_FF)),
                     _resident((D_FF, D_MODEL)),
                     _resident((D_MODEL, D_MODEL)), _resident((PLE_DIM, D_MODEL))]),
        out_specs=pl.BlockSpec((None, tm, D_MODEL), lambda b, i: (b, i, 0)),
        out_shape=jax.ShapeDtypeStruct((bsz, seq, D_MODEL), F32),
        scratch_shapes=[pltpu.VMEM((tm, D_MODEL), F32),
                        pltpu.VMEM((ext, D_MODEL), BF16),
                        pltpu.VMEM((2, ext, FF_CHUNK), F32),
                        pltpu.VMEM((2, ext, FF_CHUNK), F32),
                        pltpu.VMEM((tm, D_MODEL), F32)],
        compiler_params=pltpu.CompilerParams(
            dimension_semantics=("parallel", "arbitrary"), vmem_limit_bytes=VMEM_LIMIT_BYTES),
        name="post",
    )(h, h, h, o_attn, o_attn, o_attn, o_conv, o_conv, o_conv, p,
      lw["w_out"], lw["ffn_norm_g"], lw["w_up"], lw["ffn_conv_w"], lw["w_down"],
      lw["w_ple_gate"], lw["w_ple_proj"])


def _rope_tables(seq):
    rows = seq // GRID_W
    row = jnp.repeat(jnp.arange(rows), GRID_W).astype(F32)
    col = jnp.tile(jnp.arange(GRID_W), rows).astype(F32)
    half = HEAD_DIM // 2
    freqs = ROPE_THETA ** (-jnp.arange(0, half, 2, dtype=F32) / half)
    ang_r = row[:, None] * freqs[None, :]
    ang_c = col[:, None] * freqs[None, :]
    ang = jnp.concatenate([ang_r, ang_r, ang_c, ang_c], axis=-1)
    cos, sin = jnp.cos(ang), jnp.sin(ang)
    first = (jnp.arange(HEAD_DIM) % (HEAD_DIM // 2)) < HEAD_DIM // 4
    sin_up = jnp.where(first, -sin, 0.0)
    sin_dn = jnp.where(first, 0.0, sin)
    widen = lambda t: jnp.tile(t, (1, LANES // HEAD_DIM))
    return widen(cos), widen(sin_up), widen(sin_dn)


def _layer_weights(i, attn_norm_g, w_in, q_norm_g, k_norm_g, mix_conv_w, attn_out_g, conv_out_g,
                   w_out, ffn_norm_g, w_up, ffn_conv_w, w_down, w_ple_gate, w_ple_proj):
    head = jnp.arange(ATTN_W) // HEAD_DIM
    q_scale = math.log2(math.e) / math.sqrt(HEAD_DIM)
    return {
        "attn_norm_g": attn_norm_g[i][None],
        "w_in": w_in[i].astype(BF16),
        "q_gain": jnp.tile(q_norm_g[i] * q_scale, N_HEADS)[None],
        "k_gain": jnp.tile(k_norm_g[i], N_KV_HEADS)[None],
        "scores_bounded": (HEAD_DIM * ROUNDING_SLACK * jnp.max(jnp.abs(q_norm_g[i] * q_scale))
                           * jnp.max(jnp.abs(k_norm_g[i])) <= SCORE_BOUND),
        "head_group": jnp.where(head[:, None] == head[None, :], 1.0 / HEAD_DIM, 0.0).astype(BF16),
        "mix_conv_w": mix_conv_w[i],
        "attn_out_g": attn_out_g[i][None],
        "conv_out_g": conv_out_g[i][None],
        "w_out": w_out[i].astype(BF16),
        "ffn_norm_g": ffn_norm_g[i][None],
        "w_up": w_up[i].astype(BF16),
        "ffn_conv_w": ffn_conv_w[i],
        "w_down": w_down[i].astype(BF16),
        "w_ple_gate": w_ple_gate[i].astype(BF16),
        "w_ple_proj": w_ple_proj[i].astype(BF16),
    }


def _trunk(x, p, layers):
    tables = _rope_tables(x.shape[1])
    h = x
    for i, lw in enumerate(layers):
        qt, k, vt, o_conv = _mixer_in(h, lw, tables)
        o_attn = lax.cond(lw["scores_bounded"],
                          functools.partial(_attention, bounded=True),
                          functools.partial(_attention, bounded=False),
                          qt, k, vt, lw["attn_out_g"])
        h = _post(h, o_attn, o_conv, p, i, lw)
    return h


def kernel(x_prompt, x_sample, p_prompt, p_sample, attn_norm_g, w_in, q_norm_g, k_norm_g, mix_conv_w,
           attn_out_g, conv_out_g, w_out, ffn_norm_g, w_up, ffn_conv_w, w_down, w_ple_gate, w_ple_proj):
    depth = w_in.shape[0]
    layers = [_layer_weights(i, attn_norm_g, w_in, q_norm_g, k_norm_g, mix_conv_w, attn_out_g,
                             conv_out_g, w_out, ffn_norm_g, w_up, ffn_conv_w, w_down, w_ple_gate,
                             w_ple_proj) for i in range(depth)]
    return (_trunk(x_prompt, p_prompt, layers), _trunk(x_sample, p_sample, layers))
```
